```python
import math
import jax, jax.numpy as jnp
from jax import lax
import numpy as np

D_MODEL = 2048
BATCH = 8
SEQ = 4096
DEPTH = 4

N_MIXERS = 4
N_A = (DEPTH + 3) // 4
N_B = (DEPTH + 2) // 4
N_C = (DEPTH + 1) // 4
N_D = DEPTH // 4
EPS = 1e-6

ATT_HEADS = 16
ATT_HEAD_DIM = 128
Q_LORA = 512
KV_LORA = 256
IDX_HEADS = 16
IDX_DIM = 64
TOPK_MAX = 256
Q_BLOCK = 128
A_IN = Q_LORA + KV_LORA + IDX_DIM + IDX_HEADS
ATT_SCALE = ATT_HEAD_DIM ** -0.5
REL_BUCKETS = 32
REL_MAX_DIST = 128
HG_HEADS = 16
HG_DK = D_MODEL // HG_HEADS
HG_DV = D_MODEL // HG_HEADS
HG_CHUNK = 64
POOL_WINDOWS = (2, 4, 8, 16)
POOL_GROUP = D_MODEL // len(POOL_WINDOWS)
CONV_WIDTH = 31
D_FF = 5632
FFN_CONV = 3

kernel_name = 'hybrid_dsa_hgrn2_pool_conformer_trunk'


def rmsnorm(x, g):
    xf = x.astype(jnp.float32)
    y = xf * lax.rsqrt(jnp.mean(xf * xf, axis=-1, keepdims=True) + EPS)
    return (y * g.astype(jnp.float32)).astype(x.dtype)


def layernorm(x, g, b):
    xf = x.astype(jnp.float32)
    mu = jnp.mean(xf, axis=-1, keepdims=True)
    xc = xf - mu
    y = xc * lax.rsqrt(jnp.mean(xc * xc, axis=-1, keepdims=True) + EPS)
    return (y * g.astype(jnp.float32) + b.astype(jnp.float32)).astype(x.dtype)


def causal_depthwise_conv(x, w, b):
    width = w.shape[0]
    y = lax.conv_general_dilated(
        x, w[:, None, :].astype(x.dtype), window_strides=(1,),
        padding=[(width - 1, 0)], dimension_numbers=('NWC', 'WIO', 'NWC'),
        feature_group_count=x.shape[-1])
    return y + b.astype(x.dtype)


def t5_bucket(dist):
    max_exact = REL_BUCKETS // 2
    n = jnp.maximum(dist, 0)
    nf = jnp.maximum(n, 1).astype(jnp.float32)
    large = max_exact + (jnp.log(nf / max_exact) / math.log(REL_MAX_DIST / max_exact)
                         * (REL_BUCKETS - max_exact)).astype(jnp.int32)
    large = jnp.minimum(large, REL_BUCKETS - 1)
    return jnp.where(n < max_exact, n, large)


def dsa_mixer(h, rel_bias, w_in, g_q, g_kv, w_uq, w_qidx, w_uk, w_uv, w_o):
    bsz, seq, _ = h.shape
    topk = min(TOPK_MAX, seq // 4)
    nb = seq // Q_BLOCK
    c_q, c_kv, k_idx, w_idx = jnp.split(
        h @ w_in, [Q_LORA, Q_LORA + KV_LORA, Q_LORA + KV_LORA + IDX_DIM], axis=-1)
    c_q = rmsnorm(c_q, g_q)
    c_kv = rmsnorm(c_kv, g_kv)
    q = (c_q @ w_uq).reshape(bsz, seq, ATT_HEADS, ATT_HEAD_DIM)
    q_lat = jnp.einsum('bshd,hdc->bshc', q, w_uk)
    q_idx = (c_q @ w_qidx).reshape(bsz, seq, IDX_HEADS, IDX_DIM)
    w_idx = w_idx * (IDX_HEADS ** -0.5)
    key_pos = jnp.arange(seq, dtype=jnp.int32)

    def to_blocks(t):
        return t.reshape((bsz, nb, Q_BLOCK) + t.shape[2:]).swapaxes(0, 1)

    def block(args):
        qb_lat, qb_idx, wb_idx, start = args
        q_pos = start + jnp.arange(Q_BLOCK, dtype=jnp.int32)
        causal = key_pos[None, :] <= q_pos[:, None]
        idx_logits = jnp.einsum('bqhd,bsd->bqhs', qb_idx, k_idx) * (IDX_DIM ** -0.5)
        score = jnp.einsum('bqhs,bqh->bqs', jax.nn.relu(idx_logits), wb_idx)
        score = jnp.where(causal[None], score.astype(jnp.float32), -jnp.inf)
        _, sel = lax.top_k(score, topk)
        valid = sel <= q_pos[None, :, None]
        kv_sel = jax.vmap(lambda c, i: c[i])(c_kv, sel)
        logits = jnp.einsum('bqhc,bqkc->bqhk', qb_lat, kv_sel).astype(jnp.float32) * ATT_SCALE
        bias = rel_bias[t5_bucket(q_pos[None, :, None] - sel)]
        logits = logits + jnp.moveaxis(bias, -1, 2).astype(jnp.float32)
        logits = jnp.where(valid[:, :, None, :], logits, -jnp.inf)
        p = jax.nn.softmax(logits, axis=-1).astype(kv_sel.dtype)
        return jnp.einsum('bqhk,bqkc->bqhc', p, kv_sel)

    starts = jnp.arange(nb, dtype=jnp.int32) * Q_BLOCK
    o_lat = lax.map(block, (to_blocks(q_lat), to_blocks(q_idx), to_blocks(w_idx), starts))
    o_lat = o_lat.swapaxes(0, 1).reshape(bsz, seq, ATT_HEADS, KV_LORA)
    o = jnp.einsum('bshc,hcd->bshd', o_lat, w_uv).reshape(bsz, seq, ATT_HEADS * ATT_HEAD_DIM)
    return o @ w_o


def hgrn2_mixer(h, w_in, lower_bound, g_norm, w_o):
    bsz, seq, _ = h.shape
    nc = seq // HG_CHUNK
    q, f, i_in, g = jnp.split(h @ w_in, 4, axis=-1)
    lb = lower_bound.astype(jnp.float32)
    f_gate = lb + (1.0 - lb) * jax.nn.sigmoid(f.astype(jnp.float32))
    log_f = jnp.log(f_gate)
    k = 1.0 - f_gate
    q = jax.nn.silu(q.astype(jnp.float32))
    v = i_in.astype(jnp.float32)

    def to_chunks(t, d):
        return t.reshape(bsz, nc, HG_CHUNK, HG_HEADS, d).transpose(1, 0, 3, 2, 4)

    mask = jnp.tril(jnp.ones((HG_CHUNK, HG_CHUNK), dtype=bool))[:, :, None]

    def step(state, inp):
        qc, kc, vc, gc = inp
        a = jnp.cumsum(gc, axis=2)
        diff = a[:, :, :, None, :] - a[:, :, None, :, :]
        decay = jnp.exp(jnp.where(mask, diff, -jnp.inf))
        scores = jnp.einsum('bhtc,bhsc,bhtsc->bhts', qc, kc, decay)
        a_last = a[:, :, -1:, :]
        out = (jnp.einsum('bhts,bhsv->bhtv', scores, vc)
               + jnp.einsum('bhtc,bhcv->bhtv', qc * jnp.exp(a), state))
        new_state = (jnp.exp(a_last[:, :, 0, :, None]) * state
                     + jnp.einsum('bhsc,bhsv->bhcv', kc * jnp.exp(a_last - a), vc))
        return new_state, out

    state0 = jnp.zeros((bsz, HG_HEADS, HG_DK, HG_DV), jnp.float32)
    _, o = lax.scan(step, state0, (to_chunks(q, HG_DK), to_chunks(k, HG_DK),
                                   to_chunks(v, HG_DV), to_chunks(log_f, HG_DK)))
    o = o.transpose(1, 0, 3, 2, 4).reshape(bsz, seq, HG_HEADS, HG_DV)
    o = o * lax.rsqrt(jnp.mean(o * o, axis=-1, keepdims=True) + EPS)
    o = o.reshape(bsz, seq, HG_HEADS * HG_DV) * g_norm.astype(jnp.float32)
    o = (o * jax.nn.silu(g.astype(jnp.float32))).astype(h.dtype)
    return o @ w_o


def pool_mixer(h, w_group, scale):
    bsz, seq, d = h.shape
    hf = h.astype(jnp.float32)
    cs = jnp.cumsum(hf, axis=1)
    cs_pad = jnp.concatenate([jnp.zeros((bsz, 1, d), jnp.float32), cs], axis=1)
    pos = jnp.arange(seq, dtype=jnp.int32)
    pooled = []
    for gi, win in enumerate(POOL_WINDOWS):
        sl = slice(gi * POOL_GROUP, (gi + 1) * POOL_GROUP)
        lo = jnp.maximum(pos + 1 - win, 0)
        window_sum = cs[:, :, sl] - cs_pad[:, lo, sl]
        count = jnp.minimum(pos + 1, win).astype(jnp.float32)
        pooled.append(window_sum / count[None, :, None])
    diff = jnp.concatenate(pooled, axis=-1) - hf
    y = jnp.einsum('bsgc,gcd->bsgd', diff.reshape(bsz, seq, len(POOL_WINDOWS), POOL_GROUP),
                   w_group.astype(jnp.float32)).reshape(bsz, seq, d)
    return (y * scale.astype(jnp.float32)).astype(h.dtype)


def conformer_conv(h, w_pw1, b_pw1, w_dw, b_dw, ln_g, ln_b, w_pw2, b_pw2):
    a, gate = jnp.split(h @ w_pw1 + b_pw1, 2, axis=-1)
    u = a * jax.nn.sigmoid(gate)
    u = causal_depthwise_conv(u, w_dw, b_dw)
    u = jax.nn.silu(layernorm(u, ln_g, ln_b))
    return u @ w_pw2 + b_pw2


def conv_ffn(h, w_up, w_conv, b_conv, w_down):
    u = causal_depthwise_conv(h @ w_up, w_conv, b_conv)
    a, b = jnp.split(u, 2, axis=-1)
    return (jax.nn.silu(a) * b) @ w_down


def setup_inputs(seed: int = 0) -> dict:
    key = jax.random.key(seed)
    ks = iter(jax.random.split(key, 64))
    D = D_MODEL

    def nrm(shape, scale):
        return jax.random.normal(next(ks), shape, jnp.float32) * scale

    def gain(shape):
        return 1.0 + nrm(shape, 0.02)

    return {
        'x': nrm((BATCH, SEQ, D), 1.0),
        'rel_bias': nrm((REL_BUCKETS, ATT_HEADS), 0.2),
        'a_w_in': nrm((N_A, D, A_IN), D ** -0.5),
        'a_g_q': gain((N_A, Q_LORA)),
        'a_g_kv': gain((N_A, KV_LORA)),
        'a_w_uq': nrm((N_A, Q_LORA, ATT_HEADS * ATT_HEAD_DIM), Q_LORA ** -0.5),
        'a_w_qidx': nrm((N_A, Q_LORA, IDX_HEADS * IDX_DIM), Q_LORA ** -0.5),
        'a_w_uk': nrm((N_A, ATT_HEADS, ATT_HEAD_DIM, KV_LORA), KV_LORA ** -0.5),
        'a_w_uv': nrm((N_A, ATT_HEADS, KV_LORA, ATT_HEAD_DIM), KV_LORA ** -0.5),
        'a_w_o': nrm((N_A, ATT_HEADS * ATT_HEAD_DIM, D), (ATT_HEADS * ATT_HEAD_DIM) ** -0.5),
        'b_w_in': nrm((N_B, D, 4 * D), D ** -0.5),
        'b_lower_bounds': nrm((DEPTH, HG_HEADS * HG_DK), 0.1),
        'b_g_norm': gain((N_B, HG_HEADS * HG_DV)),
        'b_w_o': nrm((N_B, HG_HEADS * HG_DV, D), D ** -0.5),
        'c_w_group': nrm((N_C, len(POOL_WINDOWS), POOL_GROUP, POOL_GROUP), POOL_GROUP ** -0.5),
        'c_scale': gain((N_C, D)),
        'd_w_pw1': nrm((N_D, D, 2 * D), D ** -0.5),
        'd_b_pw1': nrm((N_D, 2 * D), 0.02),
        'd_w_dw': nrm((N_D, CONV_WIDTH, D), CONV_WIDTH ** -0.5),
        'd_b_dw': nrm((N_D, D), 0.02),
        'd_ln_g': gain((N_D, D)),
        'd_ln_b': nrm((N_D, D), 0.02),
        'd_w_pw2': nrm((N_D, D, D), D ** -0.5),
        'd_b_pw2': nrm((N_D, D), 0.02),
        'norm_mix': gain((DEPTH, D)),
        'norm_ffn': gain((DEPTH, D)),
        'ffn_w_up': nrm((DEPTH, D, 2 * D_FF), D ** -0.5),
        'ffn_w_conv': nrm((DEPTH, FFN_CONV, 2 * D_FF), FFN_CONV ** -0.5),
        'ffn_b_conv': nrm((DEPTH, 2 * D_FF), 0.02),
        'ffn_w_down': nrm((DEPTH, D_FF, D), D_FF ** -0.5),
        'final_norm': gain((D,)),
    }


def reference(x, rel_bias, a_w_in, a_g_q, a_g_kv, a_w_uq, a_w_qidx, a_w_uk, a_w_uv, a_w_o,
              b_w_in, b_lower_bounds, b_g_norm, b_w_o, c_w_group, c_scale,
              d_w_pw1, d_b_pw1, d_w_dw, d_b_dw, d_ln_g, d_ln_b, d_w_pw2, d_b_pw2,
              norm_mix, norm_ffn, ffn_w_up, ffn_w_conv, ffn_b_conv, ffn_w_down, final_norm):
    lb_soft = jax.nn.softmax(b_lower_bounds.astype(jnp.float32), axis=0)
    lb_all = jnp.cumsum(lb_soft, axis=0) - lb_soft[0]
    h = x
    for i in range(DEPTH):
        j = i // N_MIXERS
        kind = i % N_MIXERS
        u = rmsnorm(h, norm_mix[i])
        if kind == 0:
            m = dsa_mixer(u, rel_bias, a_w_in[j], a_g_q[j], a_g_kv[j], a_w_uq[j], a_w_qidx[j],
                          a_w_uk[j], a_w_uv[j], a_w_o[j])
        elif kind == 1:
            m = hgrn2_mixer(u, b_w_in[j], lb_all[i], b_g_norm[j], b_w_o[j])
        elif kind == 2:
            m = pool_mixer(u, c_w_group[j], c_scale[j])
        else:
            m = conformer_conv(u, d_w_pw1[j], d_b_pw1[j], d_w_dw[j], d_b_dw[j], d_ln_g[j],
                               d_ln_b[j], d_w_pw2[j], d_b_pw2[j])
        h = h + m.astype(h.dtype)
        f = conv_ffn(rmsnorm(h, norm_ffn[i]), ffn_w_up[i], ffn_w_conv[i], ffn_b_conv[i], ffn_w_down[i])
        h = h + f.astype(h.dtype)
    return rmsnorm(h, final_norm)
```

```python
import functools
import math

import jax
import jax.numpy as jnp
import numpy as np
from jax import lax
from jax.experimental import pallas as pl
from jax.experimental.pallas import tpu as pltpu

F32 = jnp.float32
BF16 = jnp.bfloat16

EPS = 1e-6
ATT_HEADS = 16
ATT_HEAD_DIM = 128
Q_LORA = 512
KV_LORA = 256
IDX_HEADS = 16
IDX_DIM = 64
TOPK_MAX = 256
REL_BUCKETS = 32
REL_MAX_DIST = 128
HG_HEADS = 16
POOL_WINDOWS = (2, 4, 8, 16)

LANES = 128
VMEM_LIMIT = 56 * 1024 * 1024
MASKED = -1e30
INT_MIN = -2 ** 31

A_IN_PAD = 896
DSA_TQ = 128
DSA_TK = 256
HG_SUB = 16


def _dot(a, b):
    return jnp.dot(a, b, preferred_element_type=F32)


def _dot_nt(a, b):
    return lax.dot_general(a, b, (((1,), (1,)), ((), ())), preferred_element_type=F32)


def _dot_tn(a, b):
    return lax.dot_general(a, b, (((0,), (0,)), ((), ())), preferred_element_type=F32)


def _rms(x, g):
    return x * lax.rsqrt(jnp.mean(x * x, axis=-1, keepdims=True) + EPS) * g


def _sigmoid(x):
    return 1.0 / (1.0 + jnp.exp(-x))


def _silu(x):
    return x * _sigmoid(x)


def _not_first(tile, tiles_per_seq):
    return jnp.where(tile % tiles_per_seq != 0, 1.0, 0.0).astype(F32)


def _params(*sem):
    return pltpu.CompilerParams(dimension_semantics=sem, vmem_limit_bytes=VMEM_LIMIT)


def _const_spec(shape):
    nd = len(shape)
    return pl.BlockSpec(shape, lambda *_: (0,) * nd)


FFN_HALO = 16


def _ffn_kernel(h_ref, halo_ref, g_ref, wa_ref, wb_ref, ca_ref, cb_ref, ba_ref, bb_ref, wd_ref,
                gf_ref, o_ref, xn_ref, acc_ref, *, tiles_per_seq, final_norm):
    i = pl.program_id(0)
    f = pl.program_id(1)
    tm = h_ref.shape[0]

    @pl.when(f == 0)
    def _():
        g = g_ref[...]
        xn_ref[FFN_HALO:, :] = _rms(h_ref[...], g).astype(BF16)
        hn = _rms(halo_ref[...], g) * _not_first(i, tiles_per_seq)
        xn_ref[:FFN_HALO, :] = hn.astype(BF16)
        acc_ref[...] = jnp.zeros_like(acc_ref)

    xn = xn_ref[...]
    width = ca_ref.shape[0]

    def conv(u, w_ref, b_ref):
        w = w_ref[...]
        out = b_ref[...] + w[width - 1:width, :] * u[FFN_HALO:, :]
        for j in range(width - 1):
            shift = width - 1 - j
            out = out + w[j:j + 1, :] * pltpu.roll(u, shift, axis=0)[FFN_HALO:, :]
        return out

    a = conv(_dot(xn, wa_ref[...]), ca_ref, ba_ref)
    b = conv(_dot(xn, wb_ref[...]), cb_ref, bb_ref)
    gated = (_silu(a) * b).astype(BF16)
    acc_ref[...] += _dot(gated, wd_ref[...])

    @pl.when(f == pl.num_programs(1) - 1)
    def _():
        out = h_ref[...] + acc_ref[...]
        if final_norm:
            out = _rms(out, gf_ref[...])
        o_ref[...] = out


def _ffn_layer(h, g, w_up, w_conv, b_conv, w_down, gf, *, seq, final_norm, tm=512, tf=512):
    tokens, d = h.shape
    d_ff = w_down.shape[0]
    tm = min(tm, seq)
    nf = d_ff // tf
    width = w_conv.shape[0]
    assert d_ff % tf == 0 and seq % tm == 0 and width - 1 <= FFN_HALO
    kern = functools.partial(_ffn_kernel, tiles_per_seq=seq // tm, final_norm=final_norm)
    return pl.pallas_call(
        kern,
        grid=(tokens // tm, nf),
        in_specs=[
            pl.BlockSpec((tm, d), lambda i, f: (i, 0)),
            pl.BlockSpec((FFN_HALO, d), lambda i, f: (jnp.maximum(i * (tm // FFN_HALO) - 1, 0), 0)),
            pl.BlockSpec((1, d), lambda i, f: (0, 0)),
            pl.BlockSpec((d, tf), lambda i, f: (0, f)),
            pl.BlockSpec((d, tf), lambda i, f: (0, nf + f)),
            pl.BlockSpec((width, tf), lambda i, f: (0, f)),
            pl.BlockSpec((width, tf), lambda i, f: (0, nf + f)),
            pl.BlockSpec((1, tf), lambda i, f: (0, f)),
            pl.BlockSpec((1, tf), lambda i, f: (0, nf + f)),
            pl.BlockSpec((tf, d), lambda i, f: (f, 0)),
            pl.BlockSpec((1, d), lambda i, f: (0, 0)),
        ],
        out_specs=pl.BlockSpec((tm, d), lambda i, f: (i, 0)),
        out_shape=jax.ShapeDtypeStruct((tokens, d), F32),
        scratch_shapes=[pltpu.VMEM((tm + FFN_HALO, d), BF16), pltpu.VMEM((tm, d), F32)],
        compiler_params=_params("parallel", "arbitrary"),
        name="conv_ffn",
    )(h, h, g.reshape(1, d), w_up, w_up, w_conv, w_conv, b_conv.reshape(1, -1),
      b_conv.reshape(1, -1), w_down, gf.reshape(1, d))


def _dsa_pre_kernel(h_ref, g_ref, win_ref, gq_ref, gkv_ref, wuq_ref, wuk_ref, wqi_ref,
                    qlat_ref, qidx_ref, ckv_ref, kidx_ref, tail_ref):
    xn = _rms(h_ref[...], g_ref[...]).astype(BF16)
    p = _dot(xn, win_ref[...])
    cq = _rms(p[:, :Q_LORA], gq_ref[...]).astype(BF16)
    ckv_ref[...] = _rms(p[:, Q_LORA:Q_LORA + KV_LORA], gkv_ref[...]).astype(BF16)
    tail = p[:, Q_LORA + KV_LORA:]
    lane = lax.broadcasted_iota(jnp.int32, tail.shape, 1)
    kidx_ref[...] = jnp.where(lane < IDX_DIM, tail, 0.0).astype(BF16)
    tail_ref[...] = tail
    q = _dot(cq, wuq_ref[...]).astype(BF16)
    qi = _dot(cq, wqi_ref[...])
    for hd in range(ATT_HEADS):
        sl = slice(hd * ATT_HEAD_DIM, (hd + 1) * ATT_HEAD_DIM)
        qlat_ref[0, hd] = (_dot(q[:, sl], wuk_ref[hd]) * (ATT_HEAD_DIM ** -0.5)).astype(BF16)
        qidx_ref[0, hd] = qi[:, sl].astype(BF16)


def _dsa_pre(h, g, w_in, g_q, g_kv, w_uq, w_uk, w_qidx):
    tokens, d = h.shape
    tm = DSA_TQ
    nblk = tokens // tm
    hq = ATT_HEADS * ATT_HEAD_DIM
    return pl.pallas_call(
        _dsa_pre_kernel,
        grid=(nblk,),
        in_specs=[
            pl.BlockSpec((tm, d), lambda i: (i, 0)),
            _const_spec((1, d)),
            _const_spec((d, A_IN_PAD)),
            _const_spec((1, Q_LORA)),
            _const_spec((1, KV_LORA)),
            _const_spec((Q_LORA, hq)),
            _const_spec((ATT_HEADS, ATT_HEAD_DIM, KV_LORA)),
            _const_spec((Q_LORA, IDX_HEADS * LANES)),
        ],
        out_specs=[
            pl.BlockSpec((1, ATT_HEADS, tm, KV_LORA), lambda i: (i, 0, 0, 0)),
            pl.BlockSpec((1, IDX_HEADS, tm, LANES), lambda i: (i, 0, 0, 0)),
            pl.BlockSpec((tm, KV_LORA), lambda i: (i, 0)),
            pl.BlockSpec((tm, LANES), lambda i: (i, 0)),
            pl.BlockSpec((tm, LANES), lambda i: (i, 0)),
        ],
        out_shape=[
            jax.ShapeDtypeStruct((nblk, ATT_HEADS, tm, KV_LORA), BF16),
            jax.ShapeDtypeStruct((nblk, IDX_HEADS, tm, LANES), BF16),
            jax.ShapeDtypeStruct((tokens, KV_LORA), BF16),
            jax.ShapeDtypeStruct((tokens, LANES), BF16),
            jax.ShapeDtypeStruct((tokens, LANES), F32),
        ],
        compiler_params=_params("parallel"),
        name="dsa_pre",
    )(h, g.reshape(1, d), w_in, g_q.reshape(1, -1), g_kv.reshape(1, -1), w_uq, w_uk, w_qidx)


def _t5_bucket_np(dist):
    max_exact = REL_BUCKETS // 2
    n = np.maximum(dist, 0)
    nf = np.maximum(n, 1).astype(np.float32)
    large = max_exact + (np.log(nf / np.float32(max_exact)) / np.float32(math.log(REL_MAX_DIST / max_exact))
                         * np.float32(REL_BUCKETS - max_exact)).astype(np.int32)
    large = np.minimum(large, REL_BUCKETS - 1)
    return np.where(n < max_exact, n, large).astype(np.int32)


def _bias_tile_buckets():
    i = np.arange(DSA_TQ)[:, None]
    j = np.arange(DSA_TQ)[None, :]
    d0 = _t5_bucket_np(i - j)
    d1 = _t5_bucket_np(DSA_TQ + i - j)
    far = _t5_bucket_np(np.full((DSA_TQ, DSA_TQ), 2 * DSA_TQ + 1))
    assert (far == _t5_bucket_np(np.full((DSA_TQ, DSA_TQ), 10 ** 6))).all()
    kinds = [np.concatenate(p, axis=1) for p in ((far, far), (d0, far), (far, d1), (d1, d0))]
    return np.stack(kinds).astype(np.int32)


def _bias_expand_kernel(idx_ref, rb_ref, o_ref):
    hd = pl.program_id(1)
    idx = idx_ref[0]
    out = jnp.zeros(idx.shape, F32)
    for b in range(REL_BUCKETS):
        out = jnp.where(idx == b, rb_ref[b, hd], out)
    o_ref[0, 0] = out


def _bias_tiles(rel_bias):
    idx = jnp.asarray(_bias_tile_buckets())
    return pl.pallas_call(
        _bias_expand_kernel,
        grid=(4, ATT_HEADS),
        in_specs=[
            pl.BlockSpec((1, DSA_TQ, DSA_TK), lambda k, hd: (k, 0, 0)),
            pl.BlockSpec(memory_space=pltpu.SMEM),
        ],
        out_specs=pl.BlockSpec((1, 1, DSA_TQ, DSA_TK), lambda k, hd: (k, hd, 0, 0)),
        out_shape=jax.ShapeDtypeStruct((4, ATT_HEADS, DSA_TQ, DSA_TK), F32),
        name="dsa_bias_tiles",
    )(idx, rel_bias)


def _sortable(x):
    bits = lax.bitcast_convert_type(x, jnp.int32)
    return jnp.where(bits < 0, bits ^ jnp.int32(0x7FFFFFFF), bits)


def _dsa_attn_kernel(qlat_ref, qidx_ref, tail_ref, ckv_ref, kidx_ref, bias_ref, o_ref,
                     keys_ref, wb_ref, m_ref, l_ref, acc_ref, *, topk):
    qb = pl.program_id(1)
    tq, tk = DSA_TQ, DSA_TK
    rows = ATT_HEADS * tq
    n_kt = qb // (tk // tq) + 1
    q_pos = qb * tq + lax.broadcasted_iota(jnp.int32, (tq, tk), 0)
    k_off = lax.broadcasted_iota(jnp.int32, (tq, tk), 1)

    tail = tail_ref[...]
    w_scale = (IDX_HEADS ** -0.5) * (IDX_DIM ** -0.5)
    for hd in range(IDX_HEADS):
        col = tail[:, IDX_DIM + hd:IDX_DIM + hd + 1] * w_scale
        wb_ref[hd * tq:(hd + 1) * tq, :] = jnp.broadcast_to(col, (tq, LANES))

    def score_tile(kt, carry):
        start = pl.multiple_of(kt * tk, tk)
        kk = kidx_ref[pl.ds(start, tk), :]
        logits = jnp.maximum(_dot_nt(qidx_ref[0], kk), 0.0)
        wb = wb_ref[...]
        weighted = jnp.concatenate(
            [logits[:, c * LANES:(c + 1) * LANES] * wb for c in range(tk // LANES)], axis=1)
        score = jnp.sum(weighted.reshape(IDX_HEADS, tq, tk), axis=0)
        score = jnp.where(start + k_off <= q_pos, score, -jnp.inf)
        keys_ref[kt] = _sortable(score)
        return carry

    lax.fori_loop(0, n_kt, score_tile, 0)

    def count_ge(cand):
        def body(kt, cnt):
            ks = keys_ref[kt]
            for c in range(tk // LANES):
                cnt = cnt + (ks[:, c * LANES:(c + 1) * LANES] >= cand).astype(jnp.int32)
            return cnt
        cnt = lax.fori_loop(0, n_kt, body, jnp.zeros((tq, LANES), jnp.int32))
        return jnp.broadcast_to(jnp.sum(cnt, axis=-1, keepdims=True), (tq, LANES))

    zero = jnp.zeros((tq, LANES), jnp.int32)
    thr0 = jnp.where(count_ge(zero) >= topk, zero, jnp.full((tq, LANES), INT_MIN, jnp.int32))

    def bit_step(it, thr):
        cand = thr | jnp.left_shift(jnp.int32(1), 30 - it)
        return jnp.where(count_ge(cand) >= topk, cand, thr)

    thr = lax.fori_loop(0, 31, bit_step, thr0)

    m_ref[...] = jnp.full(m_ref.shape, MASKED, F32)
    l_ref[...] = jnp.zeros_like(l_ref)
    acc_ref[...] = jnp.zeros_like(acc_ref)
    parity = qb % 2
    reps = tk // LANES

    def attend_tile(kt, carry):
        start = pl.multiple_of(kt * tk, tk)
        kv = ckv_ref[pl.ds(start, tk), :]
        is_last = kt == n_kt - 1
        is_prev = kt == n_kt - 2
        kind = jnp.where(is_last, 1 + 2 * parity, jnp.where(is_prev, 2 * (1 - parity), 0))
        logits = _dot_nt(qlat_ref[0], kv).reshape(ATT_HEADS, tq, tk) + bias_ref[kind]
        ks = keys_ref[kt]
        sel = jnp.concatenate(
            [ks[:, c * LANES:(c + 1) * LANES] >= thr for c in range(reps)], axis=1)
        sel = jnp.logical_and(sel, start + k_off <= q_pos)
        logits = jnp.where(sel[None], logits, MASKED).reshape(rows, tk)
        m_prev = m_ref[...]
        m_new = jnp.maximum(m_prev, jnp.max(logits, axis=-1, keepdims=True))
        alpha = jnp.exp(m_prev - m_new)
        p = jnp.exp(logits - jnp.concatenate([m_new] * reps, axis=1))
        l_ref[...] = alpha * l_ref[...] + jnp.sum(p, axis=-1, keepdims=True)
        m_ref[...] = m_new
        acc_ref[...] = (acc_ref[...] * jnp.concatenate([alpha] * (KV_LORA // LANES), axis=1)
                        + _dot(p.astype(BF16), kv))
        return carry

    lax.fori_loop(0, n_kt, attend_tile, 0)

    inv = 1.0 / l_ref[...]
    out = acc_ref[...] * jnp.concatenate([inv] * (KV_LORA // LANES), axis=1)
    for hd in range(ATT_HEADS):
        o_ref[:, hd * KV_LORA:(hd + 1) * KV_LORA] = out[hd * tq:(hd + 1) * tq, :].astype(BF16)


def _dsa_attn(qlat, qidx, tail, ckv, kidx, bias, *, batch, seq):
    tq, tk = DSA_TQ, DSA_TK
    assert seq % tk == 0
    nqb = seq // tq
    rows = ATT_HEADS * tq
    topk = min(TOPK_MAX, seq // 4)
    tokens = batch * seq
    qlat = qlat.reshape(batch * nqb, rows, KV_LORA)
    qidx = qidx.reshape(batch * nqb, rows, LANES)
    bias = bias.reshape(4, ATT_HEADS, tq, tk)
    kern = functools.partial(_dsa_attn_kernel, topk=topk)
    return pl.pallas_call(
        kern,
        grid=(batch, nqb),
        in_specs=[
            pl.BlockSpec((1, rows, KV_LORA), lambda b, q: (b * nqb + q, 0, 0)),
            pl.BlockSpec((1, rows, LANES), lambda b, q: (b * nqb + q, 0, 0)),
            pl.BlockSpec((tq, LANES), lambda b, q: (b * nqb + q, 0)),
            pl.BlockSpec((seq, KV_LORA), lambda b, q: (b, 0)),
            pl.BlockSpec((seq, LANES), lambda b, q: (b, 0)),
            _const_spec((4, ATT_HEADS, tq, tk)),
        ],
        out_specs=pl.BlockSpec((tq, ATT_HEADS * KV_LORA), lambda b, q: (b * nqb + q, 0)),
        out_shape=jax.ShapeDtypeStruct((tokens, ATT_HEADS * KV_LORA), BF16),
        scratch_shapes=[
            pltpu.VMEM((seq // tk, tq, tk), jnp.int32),
            pltpu.VMEM((rows, LANES), F32),
            pltpu.VMEM((rows, LANES), F32),
            pltpu.VMEM((rows, LANES), F32),
            pltpu.VMEM((rows, KV_LORA), F32),
        ],
        compiler_params=_params("parallel", "arbitrary"),
        name="dsa_attn",
    )(qlat, qidx, tail, ckv, kidx, bias)


def _dsa_post_kernel(olat_ref, h_ref, wuv_ref, wo_ref, o_ref, heads_ref):
    for hd in range(ATT_HEADS):
        oh = _dot(olat_ref[:, hd * KV_LORA:(hd + 1) * KV_LORA], wuv_ref[hd])
        heads_ref[:, hd * ATT_HEAD_DIM:(hd + 1) * ATT_HEAD_DIM] = oh.astype(BF16)
    o_ref[...] = h_ref[...] + _dot(heads_ref[...], wo_ref[...])


def _dsa_post(olat, h, w_uv, w_o, tm=256):
    tokens, d = h.shape
    hq = ATT_HEADS * ATT_HEAD_DIM
    return pl.pallas_call(
        _dsa_post_kernel,
        grid=(tokens // tm,),
        in_specs=[
            pl.BlockSpec((tm, ATT_HEADS * KV_LORA), lambda i: (i, 0)),
            pl.BlockSpec((tm, d), lambda i: (i, 0)),
            _const_spec((ATT_HEADS, KV_LORA, ATT_HEAD_DIM)),
            _const_spec((hq, d)),
        ],
        out_specs=pl.BlockSpec((tm, d), lambda i: (i, 0)),
        out_shape=jax.ShapeDtypeStruct((tokens, d), F32),
        scratch_shapes=[pltpu.VMEM((tm, hq), BF16)],
        compiler_params=_params("parallel"),
        name="dsa_post",
    )(olat, h, w_uv, w_o)


def _dsa_layer(h, g, rel_bias, w_in, g_q, g_kv, w_uq, w_qidx, w_uk, w_uv, w_o, *, batch, seq):
    d = h.shape[1]
    w_in_p = jnp.pad(w_in, ((0, 0), (0, A_IN_PAD - w_in.shape[1]))).astype(BF16)
    w_qidx_p = jnp.pad(w_qidx.reshape(Q_LORA, IDX_HEADS, IDX_DIM),
                       ((0, 0), (0, 0), (0, LANES - IDX_DIM))).reshape(Q_LORA, IDX_HEADS * LANES)
    qlat, qidx, ckv, kidx, tail = _dsa_pre(
        h, g, w_in_p, g_q, g_kv, w_uq.astype(BF16), w_uk.astype(BF16), w_qidx_p.astype(BF16))
    bias = _bias_tiles(rel_bias)
    olat = _dsa_attn(qlat, qidx, tail, ckv, kidx, bias, batch=batch, seq=seq)
    return _dsa_post(olat, h, w_uv.astype(BF16), w_o.astype(BF16))


def _hg_proj_kernel(h_ref, g_ref, w_ref, lb_ref, o_ref, xn_ref, *, blocks_per_part):
    j = pl.program_id(1)

    @pl.when(j == 0)
    def _():
        xn_ref[...] = _rms(h_ref[...], g_ref[...]).astype(BF16)

    y = _dot(xn_ref[...], w_ref[...])
    part = j // blocks_per_part

    @pl.when(jnp.logical_or(part == 0, part == 3))
    def _():
        o_ref[...] = _silu(y)

    @pl.when(part == 1)
    def _():
        lb = lb_ref[...]
        o_ref[...] = lb + (1.0 - lb) * _sigmoid(y)

    @pl.when(part == 2)
    def _():
        o_ref[...] = y


def _hg_proj(h, g, w_in, lb, tm=512, tn=512):
    tokens, d = h.shape
    n = w_in.shape[1]
    bpp = d // tn
    kern = functools.partial(_hg_proj_kernel, blocks_per_part=bpp)
    return pl.pallas_call(
        kern,
        grid=(tokens // tm, n // tn),
        in_specs=[
            pl.BlockSpec((tm, d), lambda i, j: (i, 0)),
            pl.BlockSpec((1, d), lambda i, j: (0, 0)),
            pl.BlockSpec((d, tn), lambda i, j: (0, j)),
            pl.BlockSpec((1, tn), lambda i, j: (0, j % bpp)),
        ],
        out_specs=pl.BlockSpec((tm, tn), lambda i, j: (i, j)),
        out_shape=jax.ShapeDtypeStruct((tokens, n), F32),
        scratch_shapes=[pltpu.VMEM((tm, d), BF16)],
        compiler_params=_params("parallel", "arbitrary"),
        name="hgrn2_proj",
    )(h, g.reshape(1, d), w_in, lb.reshape(1, d))


def _hg_rec_kernel(q_ref, f_ref, v_ref, tri_ref, o_ref, a_ref, state_ref):
    tt = pl.program_id(1)
    rows, width = q_ref.shape
    dk = width // HG_HEADS
    sub = HG_SUB

    @pl.when(tt == 0)
    def _():
        state_ref[...] = jnp.zeros_like(state_ref)

    a_ref[...] = jnp.dot(tri_ref[...], jnp.log(f_ref[...]), preferred_element_type=F32,
                         precision=lax.Precision.HIGHEST)
    t_idx = lax.broadcasted_iota(jnp.int32, (sub, 1), 0)

    def step(j, carry):
        r0 = pl.multiple_of(j * sub, sub)
        a = a_ref[pl.ds(r0, sub), :]
        q = q_ref[pl.ds(r0, sub), :]
        k = 1.0 - f_ref[pl.ds(r0, sub), :]
        v = v_ref[pl.ds(r0, sub), :]
        a_last = a[sub - 1:sub, :]
        q_dec = (q * jnp.exp(a)).astype(BF16)
        k_dec = (k * jnp.exp(a_last - a)).astype(BF16)
        s_dec = jnp.exp(a_last)
        v16 = v.astype(BF16)

        intra = [jnp.zeros((sub, dk), F32) for _ in range(HG_HEADS)]
        for s in range(sub):
            e = jnp.exp(jnp.minimum(a - a[s:s + 1, :], 0.0))
            w = q * (k[s:s + 1, :] * e)
            for hd in range(HG_HEADS):
                sl = slice(hd * dk, (hd + 1) * dk)
                col = jnp.sum(w[:, sl], axis=-1, keepdims=True)
                col = jnp.where(t_idx >= s, col, 0.0)
                intra[hd] = intra[hd] + col * v[s:s + 1, sl]

        for hd in range(HG_HEADS):
            sl = slice(hd * dk, (hd + 1) * dk)
            st = state_ref[hd]
            inter = _dot_nt(q_dec[:, sl], st.astype(BF16))
            o_ref[pl.ds(r0, sub), sl] = inter + intra[hd]
            state_ref[hd] = st * s_dec[:, sl] + _dot_tn(v16[:, sl], k_dec[:, sl])
        return carry

    lax.fori_loop(0, rows // sub, step, 0)


def _hg_rec(proj, *, batch, seq, d, tt=256):
    tokens = batch * seq
    tt = min(tt, seq)
    nt = seq // tt
    dk = d // HG_HEADS
    tri = (np.arange(tt)[:, None] >= np.arange(tt)[None, :]) & (
        np.arange(tt)[:, None] // HG_SUB == np.arange(tt)[None, :] // HG_SUB)
    tri = jnp.asarray(tri.astype(np.float32))
    return pl.pallas_call(
        _hg_rec_kernel,
        grid=(batch, nt),
        in_specs=[
            pl.BlockSpec((tt, d), lambda b, t: (b * nt + t, 0)),
            pl.BlockSpec((tt, d), lambda b, t: (b * nt + t, 1)),
            pl.BlockSpec((tt, d), lambda b, t: (b * nt + t, 2)),
            _const_spec((tt, tt)),
        ],
        out_specs=pl.BlockSpec((tt, d), lambda b, t: (b * nt + t, 0)),
        out_shape=jax.ShapeDtypeStruct((tokens, d), F32),
        scratch_shapes=[pltpu.VMEM((tt, d), F32), pltpu.VMEM((HG_HEADS, dk, dk), F32)],
        compiler_params=_params("parallel", "arbitrary"),
        name="hgrn2_recurrence",
    )(proj, proj, proj, tri)


def _hg_post_kernel(o_ref, gate_ref, h_ref, gn_ref, w_ref, out_ref, y_ref):
    width = o_ref.shape[1]
    dv = width // HG_HEADS
    for hd in range(HG_HEADS):
        sl = slice(hd * dv, (hd + 1) * dv)
        o = o_ref[:, sl]
        o = o * lax.rsqrt(jnp.mean(o * o, axis=-1, keepdims=True) + EPS)
        y_ref[:, sl] = (o * gn_ref[:, sl] * gate_ref[:, sl]).astype(BF16)
    out_ref[...] = h_ref[...] + _dot(y_ref[...], w_ref[...])


def _hg_post(o, proj, h, g_norm, w_o, tm=256):
    tokens, d = h.shape
    return pl.pallas_call(
        _hg_post_kernel,
        grid=(tokens // tm,),
        in_specs=[
            pl.BlockSpec((tm, d), lambda i: (i, 0)),
            pl.BlockSpec((tm, d), lambda i: (i, 3)),
            pl.BlockSpec((tm, d), lambda i: (i, 0)),
            _const_spec((1, d)),
            _const_spec((d, d)),
        ],
        out_specs=pl.BlockSpec((tm, d), lambda i: (i, 0)),
        out_shape=jax.ShapeDtypeStruct((tokens, d), F32),
        scratch_shapes=[pltpu.VMEM((tm, d), BF16)],
        compiler_params=_params("parallel"),
        name="hgrn2_post",
    )(o, proj, h, g_norm.reshape(1, d), w_o)


def _hgrn2_layer(h, g, w_in, lb, g_norm, w_o, *, batch, seq):
    d = h.shape[1]
    proj = _hg_proj(h, g, w_in.astype(BF16), lb)
    o = _hg_rec(proj, batch=batch, seq=seq, d=d)
    return _hg_post(o, proj, h, g_norm, w_o.astype(BF16))


def _lower_bounds_kernel(b_ref, o_ref):
    b = b_ref[...]
    e = jnp.exp(b - jnp.max(b, axis=0, keepdims=True))
    soft = e / jnp.sum(e, axis=0, keepdims=True)
    run = jnp.zeros_like(soft[0:1])
    for layer in range(soft.shape[0]):
        if layer > 0:
            run = run + soft[layer:layer + 1]
        o_ref[layer:layer + 1, :] = run


def _lower_bounds(b):
    return pl.pallas_call(
        _lower_bounds_kernel,
        out_shape=jax.ShapeDtypeStruct(b.shape, F32),
        name="hgrn2_lower_bounds",
    )(b)


POOL_HALO = 16


def _pool_kernel(h_ref, halo_ref, g_ref, w_ref, sc_ref, o_ref, *, tiles_per_seq):
    i = pl.program_id(0)
    tm, d = h_ref.shape
    group = d // len(POOL_WINDOWS)
    g = g_ref[...]
    u = _rms(h_ref[...], g)
    hu = _rms(halo_ref[...], g) * _not_first(i, tiles_per_seq)
    ext = jnp.concatenate([hu, u], axis=0)
    pos = (i % tiles_per_seq) * tm + lax.broadcasted_iota(jnp.int32, (tm, group), 0)
    for gi, win in enumerate(POOL_WINDOWS):
        sl = slice(gi * group, (gi + 1) * group)
        s = ext[:, sl]
        span = 1
        while span < win:
            s = s + pltpu.roll(s, span, axis=0)
            span *= 2
        count = jnp.minimum(pos + 1, win).astype(F32)
        diff = s[POOL_HALO:, :] / count - u[:, sl]
        y = _dot(diff.astype(BF16), w_ref[gi])
        o_ref[:, sl] = h_ref[:, sl] + y * sc_ref[:, sl]


def _pool_layer(h, g, w_group, scale, *, seq, tm=256):
    tokens, d = h.shape
    tm = min(tm, seq)
    assert max(POOL_WINDOWS) <= POOL_HALO
    kern = functools.partial(_pool_kernel, tiles_per_seq=seq // tm)
    return pl.pallas_call(
        kern,
        grid=(tokens // tm,),
        in_specs=[
            pl.BlockSpec((tm, d), lambda i: (i, 0)),
            pl.BlockSpec((POOL_HALO, d), lambda i: (jnp.maximum(i * (tm // POOL_HALO) - 1, 0), 0)),
            _const_spec((1, d)),
            _const_spec(w_group.shape),
            _const_spec((1, d)),
        ],
        out_specs=pl.BlockSpec((tm, d), lambda i: (i, 0)),
        out_shape=jax.ShapeDtypeStruct((tokens, d), F32),
        compiler_params=_params("parallel"),
        name="pool_mixer",
    )(h, h, g.reshape(1, d), w_group.astype(BF16), scale.reshape(1, d))


CONF_HALO = 32


def _conf_glu_kernel(h_ref, g_ref, wa_ref, wg_ref, ba_ref, bg_ref, o_ref, xn_ref):
    @pl.when(pl.program_id(1) == 0)
    def _():
        xn_ref[...] = _rms(h_ref[...], g_ref[...]).astype(BF16)

    xn = xn_ref[...]
    a = _dot(xn, wa_ref[...]) + ba_ref[...]
    gate = _dot(xn, wg_ref[...]) + bg_ref[...]
    o_ref[...] = a * _sigmoid(gate)


def _conf_glu(h, g, w_pw1, b_pw1, tm=512, tn=512):
    tokens, d = h.shape
    nj = d // tn
    b = b_pw1.reshape(1, -1)
    return pl.pallas_call(
        _conf_glu_kernel,
        grid=(tokens // tm, nj),
        in_specs=[
            pl.BlockSpec((tm, d), lambda i, j: (i, 0)),
            pl.BlockSpec((1, d), lambda i, j: (0, 0)),
            pl.BlockSpec((d, tn), lambda i, j: (0, j)),
            pl.BlockSpec((d, tn), lambda i, j: (0, nj + j)),
            pl.BlockSpec((1, tn), lambda i, j: (0, j)),
            pl.BlockSpec((1, tn), lambda i, j: (0, nj + j)),
        ],
        out_specs=pl.BlockSpec((tm, tn), lambda i, j: (i, j)),
        out_shape=jax.ShapeDtypeStruct((tokens, d), F32),
        scratch_shapes=[pltpu.VMEM((tm, d), BF16)],
        compiler_params=_params("parallel", "arbitrary"),
        name="conformer_glu",
    )(h, g.reshape(1, d), w_pw1, w_pw1, b, b)


def _conf_conv_kernel(u_ref, halo_ref, h_ref, wdw_ref, bdw_ref, lng_ref, lnb_ref, w2_ref, b2_ref,
                      o_ref, ext_ref, c_ref, *, tiles_per_seq):
    i = pl.program_id(0)
    tm, d = u_ref.shape
    width = wdw_ref.shape[0]
    ext_ref[:CONF_HALO, :] = halo_ref[...] * _not_first(i, tiles_per_seq)
    ext_ref[CONF_HALO:, :] = u_ref[...]
    base = CONF_HALO - (width - 1)
    for c in range(d // LANES):
        sl = slice(c * LANES, (c + 1) * LANES)
        acc = jnp.broadcast_to(bdw_ref[:, sl], (tm, LANES))
        for j in range(width):
            acc = acc + wdw_ref[j:j + 1, sl] * ext_ref[base + j:base + j + tm, sl]
        c_ref[:, sl] = acc
    x = c_ref[...]
    mu = jnp.mean(x, axis=-1, keepdims=True)
    xc = x - mu
    y = xc * lax.rsqrt(jnp.mean(xc * xc, axis=-1, keepdims=True) + EPS)
    y = _silu(y * lng_ref[...] + lnb_ref[...]).astype(BF16)
    o_ref[...] = h_ref[...] + _dot(y, w2_ref[...]) + b2_ref[...]


def _conf_conv(u, h, w_dw, b_dw, ln_g, ln_b, w_pw2, b_pw2, *, seq, tm=256):
    tokens, d = h.shape
    tm = min(tm, seq)
    width = w_dw.shape[0]
    assert width - 1 <= CONF_HALO
    kern = functools.partial(_conf_conv_kernel, tiles_per_seq=seq // tm)
    return pl.pallas_call(
        kern,
        grid=(tokens // tm,),
        in_specs=[
            pl.BlockSpec((tm, d), lambda i: (i, 0)),
            pl.BlockSpec((CONF_HALO, d), lambda i: (jnp.maximum(i * (tm // CONF_HALO) - 1, 0), 0)),
            pl.BlockSpec((tm, d), lambda i: (i, 0)),
            _const_spec((width, d)),
            _const_spec((1, d)),
            _const_spec((1, d)),
            _const_spec((1, d)),
            _const_spec((d, d)),
            _const_spec((1, d)),
        ],
        out_specs=pl.BlockSpec((tm, d), lambda i: (i, 0)),
        out_shape=jax.ShapeDtypeStruct((tokens, d), F32),
        scratch_shapes=[pltpu.VMEM((tm + CONF_HALO, d), F32), pltpu.VMEM((tm, d), F32)],
        compiler_params=_params("parallel"),
        name="conformer_conv",
    )(u, u, h, w_dw, b_dw.reshape(1, d), ln_g.reshape(1, d), ln_b.reshape(1, d), w_pw2,
      b_pw2.reshape(1, d))


def _conformer_layer(h, g, w_pw1, b_pw1, w_dw, b_dw, ln_g, ln_b, w_pw2, b_pw2, *, seq):
    u = _conf_glu(h, g, w_pw1.astype(BF16), b_pw1)
    return _conf_conv(u, h, w_dw, b_dw, ln_g, ln_b, w_pw2.astype(BF16), b_pw2, seq=seq)


def kernel(x, rel_bias, a_w_in, a_g_q, a_g_kv, a_w_uq, a_w_qidx, a_w_uk, a_w_uv, a_w_o, b_w_in, b_lower_bounds, b_g_norm, b_w_o, c_w_group, c_scale, d_w_pw1, d_b_pw1, d_w_dw, d_b_dw, d_ln_g, d_ln_b, d_w_pw2, d_b_pw2, norm_mix, norm_ffn, ffn_w_up, ffn_w_conv, ffn_b_conv, ffn_w_down, final_norm):
    batch, seq, d = x.shape
    depth = norm_mix.shape[0]
    lb_all = _lower_bounds(b_lower_bounds)
    h = x.reshape(batch * seq, d)
    for i in range(depth):
        j, kind = divmod(i, 4)
        g = norm_mix[i]
        if kind == 0:
            h = _dsa_layer(h, g, rel_bias, a_w_in[j], a_g_q[j], a_g_kv[j], a_w_uq[j], a_w_qidx[j],
                           a_w_uk[j], a_w_uv[j], a_w_o[j], batch=batch, seq=seq)
        elif kind == 1:
            h = _hgrn2_layer(h, g, b_w_in[j], lb_all[i], b_g_norm[j], b_w_o[j], batch=batch, seq=seq)
        elif kind == 2:
            h = _pool_layer(h, g, c_w_group[j], c_scale[j], seq=seq)
        else:
            h = _conformer_layer(h, g, d_w_pw1[j], d_b_pw1[j], d_w_dw[j], d_b_dw[j], d_ln_g[j],
                                 d_ln_b[j], d_w_pw2[j], d_b_pw2[j], seq=seq)
        h = _ffn_layer(h, norm_ffn[i], ffn_w_up[i].astype(BF16), ffn_w_conv[i], ffn_b_conv[i],
                       ffn_w_down[i].astype(BF16), final_norm, seq=seq,
                       final_norm=(i == depth - 1))
    return h.reshape(batch, seq, d)
```

```python
import functools
import math

import jax
import jax.numpy as jnp
import numpy as np
from jax import lax
from jax.experimental import pallas as pl
from jax.experimental.pallas import tpu as pltpu

F32 = jnp.float32
BF16 = jnp.bfloat16

EPS = 1e-6
ATT_HEADS = 16
ATT_HEAD_DIM = 128
Q_LORA = 512
KV_LORA = 256
IDX_HEADS = 16
IDX_DIM = 64
TOPK_MAX = 256
REL_BUCKETS = 32
REL_MAX_DIST = 128
HG_HEADS = 16
POOL_WINDOWS = (2, 4, 8, 16)

LANES = 128
SUBLANES = 8
VMEM_LIMIT = 56 * 1024 * 1024
MASKED = -1e30
INT_MIN = -2 ** 31

A_IN_PAD = 896
DSA_TQ = 128
DSA_TK = 256
DSA_PRE_BLOCKS = 2
HG_SUB = 16
LOG2E = math.log2(math.e)


def _dot(a, b):
    return jnp.dot(a, b, preferred_element_type=F32)


def _dot_nt(a, b):
    return lax.dot_general(a, b, (((1,), (1,)), ((), ())), preferred_element_type=F32)


def _dot_tn(a, b):
    return lax.dot_general(a, b, (((0,), (0,)), ((), ())), preferred_element_type=F32)


def _rms(x, g):
    return x * lax.rsqrt(jnp.mean(x * x, axis=-1, keepdims=True) + EPS) * g


def _sigmoid(x):
    return 1.0 / (1.0 + jnp.exp(-x))


def _silu(x):
    return x * _sigmoid(x)


def _not_first(tile, tiles_per_seq):
    return jnp.where(tile % tiles_per_seq != 0, 1.0, 0.0).astype(F32)


def _params(*sem):
    return pltpu.CompilerParams(dimension_semantics=sem, vmem_limit_bytes=VMEM_LIMIT)


def _const_spec(shape):
    nd = len(shape)
    return pl.BlockSpec(shape, lambda *_: (0,) * nd)


FFN_HALO = 16


def _ffn_kernel(h_ref, halo_ref, g_ref, wa_ref, wb_ref, ca_ref, cb_ref, ba_ref, bb_ref, wd_ref,
                gf_ref, o_ref, xn_ref, acc_ref, *, tiles_per_seq, final_norm):
    i = pl.program_id(0)
    f = pl.program_id(1)
    tm = h_ref.shape[0]

    @pl.when(f == 0)
    def _():
        g = g_ref[...]
        xn_ref[FFN_HALO:, :] = _rms(h_ref[...], g).astype(BF16)
        hn = _rms(halo_ref[...], g) * _not_first(i, tiles_per_seq)
        xn_ref[:FFN_HALO, :] = hn.astype(BF16)
        acc_ref[...] = jnp.zeros_like(acc_ref)

    xn = xn_ref[...]
    width = ca_ref.shape[0]

    def conv(u, w_ref, b_ref):
        w = w_ref[...]
        out = b_ref[...] + w[width - 1:width, :] * u[FFN_HALO:, :]
        for j in range(width - 1):
            shift = width - 1 - j
            out = out + w[j:j + 1, :] * pltpu.roll(u, shift, axis=0)[FFN_HALO:, :]
        return out

    a = conv(_dot(xn, wa_ref[...]), ca_ref, ba_ref)
    b = conv(_dot(xn, wb_ref[...]), cb_ref, bb_ref)
    gated = (_silu(a) * b).astype(BF16)
    acc_ref[...] += _dot(gated, wd_ref[...])

    @pl.when(f == pl.num_programs(1) - 1)
    def _():
        out = h_ref[...] + acc_ref[...]
        if final_norm:
            out = _rms(out, gf_ref[...])
        o_ref[...] = out


def _ffn_layer(h, g, w_up, w_conv, b_conv, w_down, gf, *, seq, final_norm, tm=512, tf=512):
    tokens, d = h.shape
    d_ff = w_down.shape[0]
    tm = min(tm, seq)
    nf = d_ff // tf
    width = w_conv.shape[0]
    assert d_ff % tf == 0 and seq % tm == 0 and width - 1 <= FFN_HALO
    kern = functools.partial(_ffn_kernel, tiles_per_seq=seq // tm, final_norm=final_norm)
    return pl.pallas_call(
        kern,
        grid=(tokens // tm, nf),
        in_specs=[
            pl.BlockSpec((tm, d), lambda i, f: (i, 0)),
            pl.BlockSpec((FFN_HALO, d), lambda i, f: (jnp.maximum(i * (tm // FFN_HALO) - 1, 0), 0)),
            pl.BlockSpec((1, d), lambda i, f: (0, 0)),
            pl.BlockSpec((d, tf), lambda i, f: (0, f)),
            pl.BlockSpec((d, tf), lambda i, f: (0, nf + f)),
            pl.BlockSpec((width, tf), lambda i, f: (0, f)),
            pl.BlockSpec((width, tf), lambda i, f: (0, nf + f)),
            pl.BlockSpec((1, tf), lambda i, f: (0, f)),
            pl.BlockSpec((1, tf), lambda i, f: (0, nf + f)),
            pl.BlockSpec((tf, d), lambda i, f: (f, 0)),
            pl.BlockSpec((1, d), lambda i, f: (0, 0)),
        ],
        out_specs=pl.BlockSpec((tm, d), lambda i, f: (i, 0)),
        out_shape=jax.ShapeDtypeStruct((tokens, d), F32),
        scratch_shapes=[pltpu.VMEM((tm + FFN_HALO, d), BF16), pltpu.VMEM((tm, d), F32)],
        compiler_params=_params("parallel", "arbitrary"),
        name="conv_ffn",
    )(h, h, g.reshape(1, d), w_up, w_up, w_conv, w_conv, b_conv.reshape(1, -1),
      b_conv.reshape(1, -1), w_down, gf.reshape(1, d))


def _dsa_pre_kernel(h_ref, g_ref, win_ref, gq_ref, gkv_ref, wuq_ref, wuk_ref, wqi_ref,
                    qlat_ref, qidx_ref, ckv_ref, kidx_ref, tail_ref):
    xn = _rms(h_ref[...], g_ref[...]).astype(BF16)
    p = _dot(xn, win_ref[...])
    cq = _rms(p[:, :Q_LORA], gq_ref[...]).astype(BF16)
    ckv_ref[...] = _rms(p[:, Q_LORA:Q_LORA + KV_LORA], gkv_ref[...]).astype(BF16)
    tail = p[:, Q_LORA + KV_LORA:]
    lane = lax.broadcasted_iota(jnp.int32, tail.shape, 1)
    kidx_ref[...] = jnp.where(lane < IDX_DIM, tail, 0.0).astype(BF16)
    tail_ref[...] = tail
    q = _dot(cq, wuq_ref[...]).astype(BF16)
    qi = _dot(cq, wqi_ref[...])
    q_scale = (ATT_HEAD_DIM ** -0.5) * LOG2E
    for hd in range(ATT_HEADS):
        sl = slice(hd * ATT_HEAD_DIM, (hd + 1) * ATT_HEAD_DIM)
        ql = (_dot(q[:, sl], wuk_ref[hd]) * q_scale).astype(BF16)
        qx = qi[:, sl].astype(BF16)
        for blk in range(DSA_PRE_BLOCKS):
            rows = slice(blk * DSA_TQ, (blk + 1) * DSA_TQ)
            qlat_ref[blk, hd] = ql[rows]
            qidx_ref[blk, hd] = qx[rows]


def _dsa_pre(h, g, w_in, g_q, g_kv, w_uq, w_uk, w_qidx):
    tokens, d = h.shape
    nq = DSA_PRE_BLOCKS
    tm = nq * DSA_TQ
    nblk = tokens // DSA_TQ
    hq = ATT_HEADS * ATT_HEAD_DIM
    return pl.pallas_call(
        _dsa_pre_kernel,
        grid=(tokens // tm,),
        in_specs=[
            pl.BlockSpec((tm, d), lambda i: (i, 0)),
            _const_spec((1, d)),
            _const_spec((d, A_IN_PAD)),
            _const_spec((1, Q_LORA)),
            _const_spec((1, KV_LORA)),
            _const_spec((Q_LORA, hq)),
            _const_spec((ATT_HEADS, ATT_HEAD_DIM, KV_LORA)),
            _const_spec((Q_LORA, IDX_HEADS * LANES)),
        ],
        out_specs=[
            pl.BlockSpec((nq, ATT_HEADS, DSA_TQ, KV_LORA), lambda i: (i, 0, 0, 0)),
            pl.BlockSpec((nq, IDX_HEADS, DSA_TQ, LANES), lambda i: (i, 0, 0, 0)),
            pl.BlockSpec((tm, KV_LORA), lambda i: (i, 0)),
            pl.BlockSpec((tm, LANES), lambda i: (i, 0)),
            pl.BlockSpec((tm, LANES), lambda i: (i, 0)),
        ],
        out_shape=[
            jax.ShapeDtypeStruct((nblk, ATT_HEADS, DSA_TQ, KV_LORA), BF16),
            jax.ShapeDtypeStruct((nblk, IDX_HEADS, DSA_TQ, LANES), BF16),
            jax.ShapeDtypeStruct((tokens, KV_LORA), BF16),
            jax.ShapeDtypeStruct((tokens, LANES), BF16),
            jax.ShapeDtypeStruct((tokens, LANES), F32),
        ],
        compiler_params=_params("parallel"),
        name="dsa_pre",
    )(h, g.reshape(1, d), w_in, g_q.reshape(1, -1), g_kv.reshape(1, -1), w_uq, w_uk, w_qidx)


def _t5_bucket_np(dist):
    max_exact = REL_BUCKETS // 2
    n = np.maximum(dist, 0)
    nf = np.maximum(n, 1).astype(np.float32)
    large = max_exact + (np.log(nf / np.float32(max_exact)) / np.float32(math.log(REL_MAX_DIST / max_exact))
                         * np.float32(REL_BUCKETS - max_exact)).astype(np.int32)
    large = np.minimum(large, REL_BUCKETS - 1)
    return np.where(n < max_exact, n, large).astype(np.int32)


def _bias_tile_buckets():
    i = np.arange(DSA_TQ)[:, None]
    j = np.arange(DSA_TQ)[None, :]
    d0 = _t5_bucket_np(i - j)
    d1 = _t5_bucket_np(DSA_TQ + i - j)
    far = _t5_bucket_np(np.full((DSA_TQ, DSA_TQ), 2 * DSA_TQ + 1))
    assert (far == _t5_bucket_np(np.full((DSA_TQ, DSA_TQ), 10 ** 6))).all()
    kinds = [np.concatenate(p, axis=1) for p in ((far, far), (d0, far), (far, d1), (d1, d0))]
    return np.stack(kinds).astype(np.int32)


def _far_bucket():
    return int(_t5_bucket_np(np.array(2 * DSA_TQ + 1)))


def _bias_expand_kernel(idx_ref, rb_ref, o_ref):
    hd = pl.program_id(1)
    idx = idx_ref[0]
    out = jnp.zeros(idx.shape, F32)
    for b in range(REL_BUCKETS):
        out = jnp.where(idx == b, rb_ref[b, hd], out)
    o_ref[0, 0] = (out - rb_ref[_far_bucket(), hd]) * LOG2E


def _bias_tiles(rel_bias):
    idx = jnp.asarray(_bias_tile_buckets())
    return pl.pallas_call(
        _bias_expand_kernel,
        grid=(4, ATT_HEADS),
        in_specs=[
            pl.BlockSpec((1, DSA_TQ, DSA_TK), lambda k, hd: (k, 0, 0)),
            pl.BlockSpec(memory_space=pltpu.SMEM),
        ],
        out_specs=pl.BlockSpec((1, 1, DSA_TQ, DSA_TK), lambda k, hd: (k, hd, 0, 0)),
        out_shape=jax.ShapeDtypeStruct((4, ATT_HEADS, DSA_TQ, DSA_TK), F32),
        name="dsa_bias_tiles",
    )(idx, rel_bias)


def _sortable(x):
    bits = lax.bitcast_convert_type(x, jnp.int32)
    return jnp.where(bits < 0, bits ^ jnp.int32(0x7FFFFFFF), bits)


def _dsa_attn_kernel(qlat_ref, qidx_ref, tail_ref, ckv_ref, kidx_ref, bias_ref, o_ref,
                     keys_ref, keyst_ref, wb_ref, m_ref, l_ref, alpha_ref, acc_ref, lg_ref, p_ref, mk_ref,
                     *, topk):
    qb = pl.program_id(1)
    tq, tk = DSA_TQ, DSA_TK
    rows = ATT_HEADS * tq
    n_kt = qb // (tk // tq) + 1
    q_pos = qb * tq + lax.broadcasted_iota(jnp.int32, (tq, tk), 0)
    k_off = lax.broadcasted_iota(jnp.int32, (tq, tk), 1)

    tail = tail_ref[...]
    w_scale = (IDX_HEADS ** -0.5) * (IDX_DIM ** -0.5)
    for hd in range(IDX_HEADS):
        col = tail[:, IDX_DIM + hd:IDX_DIM + hd + 1] * w_scale
        wb_ref[hd * tq:(hd + 1) * tq, :] = jnp.broadcast_to(col, (tq, LANES))

    def score_tile(kt, carry):
        start = pl.multiple_of(kt * tk, tk)
        kk = kidx_ref[pl.ds(start, tk), :]
        logits = jnp.maximum(_dot_nt(qidx_ref[0], kk), 0.0)
        wb = wb_ref[...]
        weighted = jnp.concatenate(
            [logits[:, c * LANES:(c + 1) * LANES] * wb for c in range(tk // LANES)], axis=1)
        score = jnp.sum(weighted.reshape(IDX_HEADS, tq, tk), axis=0)
        score = jnp.where(start + k_off <= q_pos, score, -jnp.inf)
        keys_ref[kt] = _sortable(score)
        keyst_ref[kt] = _sortable(score.T)
        return carry

    lax.fori_loop(0, n_kt, score_tile, 0)

    reps = tk // LANES

    def count_ge(cand):
        def body(kt, cnt):
            ge = (keyst_ref[kt] >= cand).astype(jnp.int32)
            return cnt + jnp.sum(ge.reshape(tk // SUBLANES, SUBLANES, tq), axis=0)
        cnt = lax.fori_loop(0, n_kt, body, jnp.zeros((SUBLANES, tq), jnp.int32))
        return jnp.sum(cnt, axis=0, keepdims=True)

    zero = jnp.zeros((1, tq), jnp.int32)
    thr0 = jnp.where(count_ge(zero) >= topk, zero, jnp.full((1, tq), INT_MIN, jnp.int32))

    def bit_step(it, t):
        cand = t | jnp.left_shift(jnp.int32(1), 30 - it)
        return jnp.where(count_ge(cand) >= topk, cand, t)

    thr_row = lax.fori_loop(0, 31, bit_step, thr0)
    thr = lax.bitcast_convert_type(
        jnp.broadcast_to(lax.bitcast_convert_type(thr_row, F32), (LANES, tq)).T, jnp.int32)

    m_ref[...] = jnp.full(m_ref.shape, MASKED, F32)
    l_ref[...] = jnp.zeros_like(l_ref)
    acc_ref[...] = jnp.zeros_like(acc_ref)
    parity = qb % 2

    def head_update(hd, extra):
        r = slice(hd * tq, (hd + 1) * tq)
        lg = lg_ref[r, :] + mk_ref[...]
        if extra is not None:
            lg = lg + extra
        m_prev = m_ref[r, :]
        tile_max = jnp.maximum(lg[:, :LANES], lg[:, LANES:])
        for c in range(2, reps):
            tile_max = jnp.maximum(tile_max, lg[:, c * LANES:(c + 1) * LANES])
        m_new = jnp.maximum(m_prev, jnp.max(tile_max, axis=-1, keepdims=True))
        alpha = jnp.exp2(m_prev - m_new)
        p = jnp.exp2(lg - jnp.concatenate([m_new] * reps, axis=1))
        tile_sum = p[:, :LANES] + p[:, LANES:]
        for c in range(2, reps):
            tile_sum = tile_sum + p[:, c * LANES:(c + 1) * LANES]
        l_ref[r, :] = alpha * l_ref[r, :] + jnp.sum(tile_sum, axis=-1, keepdims=True)
        m_ref[r, :] = m_new
        alpha_ref[r, :] = alpha
        p_ref[r, :] = p.astype(BF16)

    def attend_tile(kt, near):
        start = pl.multiple_of(kt * tk, tk)
        kv = ckv_ref[pl.ds(start, tk), :]
        lg_ref[...] = _dot_nt(qlat_ref[0], kv)
        ks = keys_ref[kt]
        sel = jnp.concatenate(
            [ks[:, c * LANES:(c + 1) * LANES] >= thr for c in range(reps)], axis=1)
        sel = jnp.logical_and(sel, start + k_off <= q_pos)
        mk_ref[...] = jnp.where(sel, 0.0, MASKED)
        if near:
            kind = jnp.where(kt == n_kt - 1, 1 + 2 * parity, 2 * (1 - parity))
        for hd in range(ATT_HEADS):
            head_update(hd, bias_ref[kind, hd] if near else None)
        alpha = alpha_ref[...]
        acc_ref[...] = (acc_ref[...] * jnp.concatenate([alpha] * (KV_LORA // LANES), axis=1)
                        + _dot(p_ref[...], kv))

    n_far = jnp.maximum(n_kt - 2, 0)

    def far_tile(kt, carry):
        attend_tile(kt, False)
        return carry

    def near_tile(kt, carry):
        attend_tile(kt, True)
        return carry

    lax.fori_loop(0, n_far, far_tile, 0)
    lax.fori_loop(n_far, n_kt, near_tile, 0)

    inv = 1.0 / l_ref[...]
    out = acc_ref[...] * jnp.concatenate([inv] * (KV_LORA // LANES), axis=1)
    for hd in range(ATT_HEADS):
        o_ref[:, hd * KV_LORA:(hd + 1) * KV_LORA] = out[hd * tq:(hd + 1) * tq, :].astype(BF16)


def _dsa_attn(qlat, qidx, tail, ckv, kidx, bias, *, batch, seq):
    tq, tk = DSA_TQ, DSA_TK
    assert seq % tk == 0 and tq == LANES and tk == 2 * tq
    nqb = seq // tq
    rows = ATT_HEADS * tq
    topk = min(TOPK_MAX, seq // 4)
    tokens = batch * seq
    qlat = qlat.reshape(batch * nqb, rows, KV_LORA)
    qidx = qidx.reshape(batch * nqb, rows, LANES)
    kern = functools.partial(_dsa_attn_kernel, topk=topk)
    return pl.pallas_call(
        kern,
        grid=(batch, nqb),
        in_specs=[
            pl.BlockSpec((1, rows, KV_LORA), lambda b, q: (b * nqb + q, 0, 0)),
            pl.BlockSpec((1, rows, LANES), lambda b, q: (b * nqb + q, 0, 0)),
            pl.BlockSpec((tq, LANES), lambda b, q: (b * nqb + q, 0)),
            pl.BlockSpec((seq, KV_LORA), lambda b, q: (b, 0)),
            pl.BlockSpec((seq, LANES), lambda b, q: (b, 0)),
            _const_spec((4, ATT_HEADS, tq, tk)),
        ],
        out_specs=pl.BlockSpec((tq, ATT_HEADS * KV_LORA), lambda b, q: (b * nqb + q, 0)),
        out_shape=jax.ShapeDtypeStruct((tokens, ATT_HEADS * KV_LORA), BF16),
        scratch_shapes=[
            pltpu.VMEM((seq // tk, tq, tk), jnp.int32),
            pltpu.VMEM((seq // tk, tk, tq), jnp.int32),
            pltpu.VMEM((rows, LANES), F32),
            pltpu.VMEM((rows, LANES), F32),
            pltpu.VMEM((rows, LANES), F32),
            pltpu.VMEM((rows, LANES), F32),
            pltpu.VMEM((rows, KV_LORA), F32),
            pltpu.VMEM((rows, tk), F32),
            pltpu.VMEM((rows, tk), BF16),
            pltpu.VMEM((tq, tk), F32),
        ],
        compiler_params=_params("parallel", "arbitrary"),
        name="dsa_attn",
    )(qlat, qidx, tail, ckv, kidx, bias)


def _dsa_post_kernel(olat_ref, h_ref, wuv_ref, wo_ref, o_ref, heads_ref):
    for hd in range(ATT_HEADS):
        oh = _dot(olat_ref[:, hd * KV_LORA:(hd + 1) * KV_LORA], wuv_ref[hd])
        heads_ref[:, hd * ATT_HEAD_DIM:(hd + 1) * ATT_HEAD_DIM] = oh.astype(BF16)
    o_ref[...] = h_ref[...] + _dot(heads_ref[...], wo_ref[...])


def _dsa_post(olat, h, w_uv, w_o, tm=256):
    tokens, d = h.shape
    hq = ATT_HEADS * ATT_HEAD_DIM
    return pl.pallas_call(
        _dsa_post_kernel,
        grid=(tokens // tm,),
        in_specs=[
            pl.BlockSpec((tm, ATT_HEADS * KV_LORA), lambda i: (i, 0)),
            pl.BlockSpec((tm, d), lambda i: (i, 0)),
            _const_spec((ATT_HEADS, KV_LORA, ATT_HEAD_DIM)),
            _const_spec((hq, d)),
        ],
        out_specs=pl.BlockSpec((tm, d), lambda i: (i, 0)),
        out_shape=jax.ShapeDtypeStruct((tokens, d), F32),
        scratch_shapes=[pltpu.VMEM((tm, hq), BF16)],
        compiler_params=_params("parallel"),
        name="dsa_post",
    )(olat, h, w_uv, w_o)


def _dsa_layer(h, g, rel_bias, w_in, g_q, g_kv, w_uq, w_qidx, w_uk, w_uv, w_o, *, batch, seq):
    d = h.shape[1]
    w_in_p = jnp.pad(w_in, ((0, 0), (0, A_IN_PAD - w_in.shape[1]))).astype(BF16)
    w_qidx_p = jnp.pad(w_qidx.reshape(Q_LORA, IDX_HEADS, IDX_DIM),
                       ((0, 0), (0, 0), (0, LANES - IDX_DIM))).reshape(Q_LORA, IDX_HEADS * LANES)
    qlat, qidx, ckv, kidx, tail = _dsa_pre(
        h, g, w_in_p, g_q, g_kv, w_uq.astype(BF16), w_uk.astype(BF16), w_qidx_p.astype(BF16))
    bias = _bias_tiles(rel_bias)
    olat = _dsa_attn(qlat, qidx, tail, ckv, kidx, bias, batch=batch, seq=seq)
    return _dsa_post(olat, h, w_uv.astype(BF16), w_o.astype(BF16))


def _hg_proj_kernel(h_ref, g_ref, w_ref, lb_ref, o_ref, xn_ref, *, blocks_per_part):
    j = pl.program_id(1)

    @pl.when(j == 0)
    def _():
        xn_ref[...] = _rms(h_ref[...], g_ref[...]).astype(BF16)

    y = _dot(xn_ref[...], w_ref[...])
    part = j // blocks_per_part

    @pl.when(jnp.logical_or(part == 0, part == 3))
    def _():
        o_ref[...] = _silu(y)

    @pl.when(part == 1)
    def _():
        lb = lb_ref[...]
        o_ref[...] = lb + (1.0 - lb) * _sigmoid(y)

    @pl.when(part == 2)
    def _():
        o_ref[...] = y


def _hg_proj(h, g, w_in, lb, tm=512, tn=2048):
    tokens, d = h.shape
    n = w_in.shape[1]
    bpp = d // tn
    kern = functools.partial(_hg_proj_kernel, blocks_per_part=bpp)
    return pl.pallas_call(
        kern,
        grid=(tokens // tm, n // tn),
        in_specs=[
            pl.BlockSpec((tm, d), lambda i, j: (i, 0)),
            pl.BlockSpec((1, d), lambda i, j: (0, 0)),
            pl.BlockSpec((d, tn), lambda i, j: (0, j)),
            pl.BlockSpec((1, tn), lambda i, j: (0, j % bpp)),
        ],
        out_specs=pl.BlockSpec((tm, tn), lambda i, j: (i, j)),
        out_shape=jax.ShapeDtypeStruct((tokens, n), F32),
        scratch_shapes=[pltpu.VMEM((tm, d), BF16)],
        compiler_params=_params("parallel", "arbitrary"),
        name="hgrn2_proj",
    )(h, g.reshape(1, d), w_in, lb.reshape(1, d))


def _hg_rec_kernel(q_ref, f_ref, v_ref, tri_ref, o_ref, a_ref, state_ref):
    tt = pl.program_id(1)
    rows, width = q_ref.shape
    dk = width // HG_HEADS
    sub = HG_SUB

    @pl.when(tt == 0)
    def _():
        state_ref[...] = jnp.zeros_like(state_ref)

    a_ref[...] = jnp.dot(tri_ref[...], jnp.log(f_ref[...]), preferred_element_type=F32,
                         precision=lax.Precision.HIGHEST)
    t_idx = lax.broadcasted_iota(jnp.int32, (sub, 1), 0)

    def step(j, carry):
        r0 = pl.multiple_of(j * sub, sub)
        a = a_ref[pl.ds(r0, sub), :]
        q = q_ref[pl.ds(r0, sub), :]
        k = 1.0 - f_ref[pl.ds(r0, sub), :]
        v = v_ref[pl.ds(r0, sub), :]
        a_last = a[sub - 1:sub, :]
        q_dec = (q * jnp.exp(a)).astype(BF16)
        k_dec = (k * jnp.exp(a_last - a)).astype(BF16)
        s_dec = jnp.exp(a_last)
        v16 = v.astype(BF16)

        intra = [jnp.zeros((sub, dk), F32) for _ in range(HG_HEADS)]
        for s in range(sub):
            e = jnp.exp(jnp.minimum(a - a[s:s + 1, :], 0.0))
            w = q * (k[s:s + 1, :] * e)
            for hd in range(HG_HEADS):
                sl = slice(hd * dk, (hd + 1) * dk)
                col = jnp.sum(w[:, sl], axis=-1, keepdims=True)
                col = jnp.where(t_idx >= s, col, 0.0)
                intra[hd] = intra[hd] + col * v[s:s + 1, sl]

        for hd in range(HG_HEADS):
            sl = slice(hd * dk, (hd + 1) * dk)
            st = state_ref[hd]
            inter = _dot_nt(q_dec[:, sl], st.astype(BF16))
            o_ref[pl.ds(r0, sub), sl] = inter + intra[hd]
            state_ref[hd] = st * s_dec[:, sl] + _dot_tn(v16[:, sl], k_dec[:, sl])
        return carry

    lax.fori_loop(0, rows // sub, step, 0)


def _hg_rec(proj, *, batch, seq, d, tt=256):
    tokens = batch * seq
    tt = min(tt, seq)
    nt = seq // tt
    dk = d // HG_HEADS
    tri = (np.arange(tt)[:, None] >= np.arange(tt)[None, :]) & (
        np.arange(tt)[:, None] // HG_SUB == np.arange(tt)[None, :] // HG_SUB)
    tri = jnp.asarray(tri.astype(np.float32))
    return pl.pallas_call(
        _hg_rec_kernel,
        grid=(batch, nt),
        in_specs=[
            pl.BlockSpec((tt, d), lambda b, t: (b * nt + t, 0)),
            pl.BlockSpec((tt, d), lambda b, t: (b * nt + t, 1)),
            pl.BlockSpec((tt, d), lambda b, t: (b * nt + t, 2)),
            _const_spec((tt, tt)),
        ],
        out_specs=pl.BlockSpec((tt, d), lambda b, t: (b * nt + t, 0)),
        out_shape=jax.ShapeDtypeStruct((tokens, d), F32),
        scratch_shapes=[pltpu.VMEM((tt, d), F32), pltpu.VMEM((HG_HEADS, dk, dk), F32)],
        compiler_params=_params("parallel", "arbitrary"),
        name="hgrn2_recurrence",
    )(proj, proj, proj, tri)


def _hg_post_kernel(o_ref, gate_ref, h_ref, gn_ref, w_ref, out_ref, y_ref):
    width = o_ref.shape[1]
    dv = width // HG_HEADS
    for hd in range(HG_HEADS):
        sl = slice(hd * dv, (hd + 1) * dv)
        o = o_ref[:, sl]
        o = o * lax.rsqrt(jnp.mean(o * o, axis=-1, keepdims=True) + EPS)
        y_ref[:, sl] = (o * gn_ref[:, sl] * gate_ref[:, sl]).astype(BF16)
    out_ref[...] = h_ref[...] + _dot(y_ref[...], w_ref[...])


def _hg_post(o, proj, h, g_norm, w_o, tm=256):
    tokens, d = h.shape
    return pl.pallas_call(
        _hg_post_kernel,
        grid=(tokens // tm,),
        in_specs=[
            pl.BlockSpec((tm, d), lambda i: (i, 0)),
            pl.BlockSpec((tm, d), lambda i: (i, 3)),
            pl.BlockSpec((tm, d), lambda i: (i, 0)),
            _const_spec((1, d)),
            _const_spec((d, d)),
        ],
        out_specs=pl.BlockSpec((tm, d), lambda i: (i, 0)),
        out_shape=jax.ShapeDtypeStruct((tokens, d), F32),
        scratch_shapes=[pltpu.VMEM((tm, d), BF16)],
        compiler_params=_params("parallel"),
        name="hgrn2_post",
    )(o, proj, h, g_norm.reshape(1, d), w_o)


def _hgrn2_layer(h, g, w_in, lb, g_norm, w_o, *, batch, seq):
    d = h.shape[1]
    proj = _hg_proj(h, g, w_in.astype(BF16), lb)
    o = _hg_rec(proj, batch=batch, seq=seq, d=d)
    return _hg_post(o, proj, h, g_norm, w_o.astype(BF16))


def _lower_bounds_kernel(b_ref, o_ref):
    b = b_ref[...]
    e = jnp.exp(b - jnp.max(b, axis=0, keepdims=True))
    soft = e / jnp.sum(e, axis=0, keepdims=True)
    run = jnp.zeros_like(soft[0:1])
    for layer in range(soft.shape[0]):
        if layer > 0:
            run = run + soft[layer:layer + 1]
        o_ref[layer:layer + 1, :] = run


def _lower_bounds(b):
    return pl.pallas_call(
        _lower_bounds_kernel,
        out_shape=jax.ShapeDtypeStruct(b.shape, F32),
        name="hgrn2_lower_bounds",
    )(b)


POOL_HALO = 16


def _pool_kernel(h_ref, halo_ref, g_ref, w_ref, sc_ref, o_ref, *, tiles_per_seq):
    i = pl.program_id(0)
    tm, d = h_ref.shape
    group = d // len(POOL_WINDOWS)
    g = g_ref[...]
    u = _rms(h_ref[...], g)
    hu = _rms(halo_ref[...], g) * _not_first(i, tiles_per_seq)
    ext = jnp.concatenate([hu, u], axis=0)
    pos = (i % tiles_per_seq) * tm + lax.broadcasted_iota(jnp.int32, (tm, group), 0)
    for gi, win in enumerate(POOL_WINDOWS):
        sl = slice(gi * group, (gi + 1) * group)
        s = ext[:, sl]
        span = 1
        while span < win:
            s = s + pltpu.roll(s, span, axis=0)
            span *= 2
        count = jnp.minimum(pos + 1, win).astype(F32)
        diff = s[POOL_HALO:, :] / count - u[:, sl]
        y = _dot(diff.astype(BF16), w_ref[gi])
        o_ref[:, sl] = h_ref[:, sl] + y * sc_ref[:, sl]


def _pool_layer(h, g, w_group, scale, *, seq, tm=256):
    tokens, d = h.shape
    tm = min(tm, seq)
    assert max(POOL_WINDOWS) <= POOL_HALO
    kern = functools.partial(_pool_kernel, tiles_per_seq=seq // tm)
    return pl.pallas_call(
        kern,
        grid=(tokens // tm,),
        in_specs=[
            pl.BlockSpec((tm, d), lambda i: (i, 0)),
            pl.BlockSpec((POOL_HALO, d), lambda i: (jnp.maximum(i * (tm // POOL_HALO) - 1, 0), 0)),
            _const_spec((1, d)),
            _const_spec(w_group.shape),
            _const_spec((1, d)),
        ],
        out_specs=pl.BlockSpec((tm, d), lambda i: (i, 0)),
        out_shape=jax.ShapeDtypeStruct((tokens, d), F32),
        compiler_params=_params("parallel"),
        name="pool_mixer",
    )(h, h, g.reshape(1, d), w_group.astype(BF16), scale.reshape(1, d))


CONF_HALO = 32


def _conf_glu_kernel(h_ref, g_ref, wa_ref, wg_ref, ba_ref, bg_ref, o_ref, xn_ref):
    @pl.when(pl.program_id(1) == 0)
    def _():
        xn_ref[...] = _rms(h_ref[...], g_ref[...]).astype(BF16)

    xn = xn_ref[...]
    a = _dot(xn, wa_ref[...]) + ba_ref[...]
    gate = _dot(xn, wg_ref[...]) + bg_ref[...]
    o_ref[...] = a * _sigmoid(gate)


def _conf_glu(h, g, w_pw1, b_pw1, tm=512, tn=1024):
    tokens, d = h.shape
    nj = d // tn
    b = b_pw1.reshape(1, -1)
    return pl.pallas_call(
        _conf_glu_kernel,
        grid=(tokens // tm, nj),
        in_specs=[
            pl.BlockSpec((tm, d), lambda i, j: (i, 0)),
            pl.BlockSpec((1, d), lambda i, j: (0, 0)),
            pl.BlockSpec((d, tn), lambda i, j: (0, j)),
            pl.BlockSpec((d, tn), lambda i, j: (0, nj + j)),
            pl.BlockSpec((1, tn), lambda i, j: (0, j)),
            pl.BlockSpec((1, tn), lambda i, j: (0, nj + j)),
        ],
        out_specs=pl.BlockSpec((tm, tn), lambda i, j: (i, j)),
        out_shape=jax.ShapeDtypeStruct((tokens, d), F32),
        scratch_shapes=[pltpu.VMEM((tm, d), BF16)],
        compiler_params=_params("parallel", "arbitrary"),
        name="conformer_glu",
    )(h, g.reshape(1, d), w_pw1, w_pw1, b, b)


def _conf_conv_kernel(u_ref, halo_ref, h_ref, wdw_ref, bdw_ref, lng_ref, lnb_ref, w2_ref, b2_ref,
                      o_ref, ext_ref, c_ref, *, tiles_per_seq):
    i = pl.program_id(0)
    tm, d = u_ref.shape
    width = wdw_ref.shape[0]
    ext_ref[:CONF_HALO, :] = halo_ref[...] * _not_first(i, tiles_per_seq)
    ext_ref[CONF_HALO:, :] = u_ref[...]
    base = CONF_HALO - (width - 1)
    n_ext = tm + CONF_HALO
    for c in range(d // LANES):
        sl = slice(c * LANES, (c + 1) * LANES)
        ext = ext_ref[:, sl]
        acc = jnp.broadcast_to(bdw_ref[:, sl], (tm, LANES))
        for phase in range(SUBLANES):
            taps = [j for j in range(width) if (base + j) % SUBLANES == phase]
            if not taps:
                continue
            shifted = ext if phase == 0 else pltpu.roll(ext, n_ext - phase, axis=0)
            for j in taps:
                off = base + j - phase
                acc = acc + wdw_ref[j:j + 1, sl] * shifted[off:off + tm, :]
        c_ref[:, sl] = acc
    x = c_ref[...]
    mu = jnp.mean(x, axis=-1, keepdims=True)
    xc = x - mu
    y = xc * lax.rsqrt(jnp.mean(xc * xc, axis=-1, keepdims=True) + EPS)
    y = _silu(y * lng_ref[...] + lnb_ref[...]).astype(BF16)
    o_ref[...] = h_ref[...] + _dot(y, w2_ref[...]) + b2_ref[...]


def _conf_conv(u, h, w_dw, b_dw, ln_g, ln_b, w_pw2, b_pw2, *, seq, tm=256):
    tokens, d = h.shape
    tm = min(tm, seq)
    width = w_dw.shape[0]
    assert width - 1 <= CONF_HALO
    kern = functools.partial(_conf_conv_kernel, tiles_per_seq=seq // tm)
    return pl.pallas_call(
        kern,
        grid=(tokens // tm,),
        in_specs=[
            pl.BlockSpec((tm, d), lambda i: (i, 0)),
            pl.BlockSpec((CONF_HALO, d), lambda i: (jnp.maximum(i * (tm // CONF_HALO) - 1, 0), 0)),
            pl.BlockSpec((tm, d), lambda i: (i, 0)),
            _const_spec((width, d)),
            _const_spec((1, d)),
            _const_spec((1, d)),
            _const_spec((1, d)),
            _const_spec((d, d)),
            _const_spec((1, d)),
        ],
        out_specs=pl.BlockSpec((tm, d), lambda i: (i, 0)),
        out_shape=jax.ShapeDtypeStruct((tokens, d), F32),
        scratch_shapes=[pltpu.VMEM((tm + CONF_HALO, d), F32), pltpu.VMEM((tm, d), F32)],
        compiler_params=_params("parallel"),
        name="conformer_conv",
    )(u, u, h, w_dw, b_dw.reshape(1, d), ln_g.reshape(1, d), ln_b.reshape(1, d), w_pw2,
      b_pw2.reshape(1, d))


def _conformer_layer(h, g, w_pw1, b_pw1, w_dw, b_dw, ln_g, ln_b, w_pw2, b_pw2, *, seq):
    u = _conf_glu(h, g, w_pw1.astype(BF16), b_pw1)
    return _conf_conv(u, h, w_dw, b_dw, ln_g, ln_b, w_pw2.astype(BF16), b_pw2, seq=seq)


def kernel(x, rel_bias, a_w_in, a_g_q, a_g_kv, a_w_uq, a_w_qidx, a_w_uk, a_w_uv, a_w_o, b_w_in, b_lower_bounds, b_g_norm, b_w_o, c_w_group, c_scale, d_w_pw1, d_b_pw1, d_w_dw, d_b_dw, d_ln_g, d_ln_b, d_w_pw2, d_b_pw2, norm_mix, norm_ffn, ffn_w_up, ffn_w_conv, ffn_b_conv, ffn_w_down, final_norm):
    batch, seq, d = x.shape
    depth = norm_mix.shape[0]
    lb_all = _lower_bounds(b_lower_bounds)
    h = x.reshape(batch * seq, d)
    for i in range(depth):
        j, kind = divmod(i, 4)
        g = norm_mix[i]
        if kind == 0:
            h = _dsa_layer(h, g, rel_bias, a_w_in[j], a_g_q[j], a_g_kv[j], a_w_uq[j], a_w_qidx[j],
                           a_w_uk[j], a_w_uv[j], a_w_o[j], batch=batch, seq=seq)
        elif kind == 1:
            h = _hgrn2_layer(h, g, b_w_in[j], lb_all[i], b_g_norm[j], b_w_o[j], batch=batch, seq=seq)
        elif kind == 2:
            h = _pool_layer(h, g, c_w_group[j], c_scale[j], seq=seq)
        else:
            h = _conformer_layer(h, g, d_w_pw1[j], d_b_pw1[j], d_w_dw[j], d_b_dw[j], d_ln_g[j],
                                 d_ln_b[j], d_w_pw2[j], d_b_pw2[j], seq=seq)
        h = _ffn_layer(h, norm_ffn[i], ffn_w_up[i].astype(BF16), ffn_w_conv[i], ffn_b_conv[i],
                       ffn_w_down[i].astype(BF16), final_norm, seq=seq,
                       final_norm=(i == depth - 1))
    return h.reshape(batch, seq, d)
```

```python
import functools
import math

import jax
import jax.numpy as jnp
import numpy as np
from jax import lax
from jax.experimental import pallas as pl
from jax.experimental.pallas import tpu as pltpu

F32 = jnp.float32
BF16 = jnp.bfloat16

EPS = 1e-6
ATT_HEADS = 16
ATT_HEAD_DIM = 128
Q_LORA = 512
KV_LORA = 256
IDX_HEADS = 16
IDX_DIM = 64
TOPK_MAX = 256
REL_BUCKETS = 32
REL_MAX_DIST = 128
HG_HEADS = 16
POOL_WINDOWS = (2, 4, 8, 16)

LANES = 128
SUBLANES = 8
VMEM_LIMIT = 56 * 1024 * 1024
MASKED = -1e30
INT_MIN = -2 ** 31

A_IN_PAD = 896
DSA_TQ = 128
DSA_TK = 256
DSA_PRE_BLOCKS = 2
HG_SUB = 16
LOG2E = math.log2(math.e)


def _dot(a, b):
    return jnp.dot(a, b, preferred_element_type=F32)


def _dot_nt(a, b):
    return lax.dot_general(a, b, (((1,), (1,)), ((), ())), preferred_element_type=F32)


def _dot_tn(a, b):
    return lax.dot_general(a, b, (((0,), (0,)), ((), ())), preferred_element_type=F32)


def _rms(x, g):
    return x * lax.rsqrt(jnp.mean(x * x, axis=-1, keepdims=True) + EPS) * g


def _sigmoid(x):
    return 1.0 / (1.0 + jnp.exp(-x))


def _silu(x):
    return x * _sigmoid(x)


def _not_first(tile, tiles_per_seq):
    return jnp.where(tile % tiles_per_seq != 0, 1.0, 0.0).astype(F32)


def _params(*sem):
    return pltpu.CompilerParams(dimension_semantics=sem, vmem_limit_bytes=VMEM_LIMIT)


def _const_spec(shape):
    nd = len(shape)
    return pl.BlockSpec(shape, lambda *_: (0,) * nd)


FFN_HALO = 16


def _ffn_kernel(h_ref, halo_ref, g_ref, wa_ref, wb_ref, ca_ref, cb_ref, ba_ref, bb_ref, wd_ref,
                gf_ref, o_ref, xn_ref, acc_ref, *, tiles_per_seq, final_norm):
    i = pl.program_id(0)
    f = pl.program_id(1)
    tm = h_ref.shape[0]

    @pl.when(f == 0)
    def _():
        g = g_ref[...]
        xn_ref[FFN_HALO:, :] = _rms(h_ref[...], g).astype(BF16)
        hn = _rms(halo_ref[...], g) * _not_first(i, tiles_per_seq)
        xn_ref[:FFN_HALO, :] = hn.astype(BF16)
        acc_ref[...] = jnp.zeros_like(acc_ref)

    xn = xn_ref[...]
    width = ca_ref.shape[0]

    def conv(u, w_ref, b_ref):
        w = w_ref[...]
        out = b_ref[...] + w[width - 1:width, :] * u[FFN_HALO:, :]
        for j in range(width - 1):
            shift = width - 1 - j
            out = out + w[j:j + 1, :] * pltpu.roll(u, shift, axis=0)[FFN_HALO:, :]
        return out

    a = conv(_dot(xn, wa_ref[...]), ca_ref, ba_ref)
    b = conv(_dot(xn, wb_ref[...]), cb_ref, bb_ref)
    gated = (_silu(a) * b).astype(BF16)
    acc_ref[...] += _dot(gated, wd_ref[...])

    @pl.when(f == pl.num_programs(1) - 1)
    def _():
        out = h_ref[...] + acc_ref[...]
        if final_norm:
            out = _rms(out, gf_ref[...])
        o_ref[...] = out


def _ffn_layer(h, g, w_up, w_conv, b_conv, w_down, gf, *, seq, final_norm, tm=512, tf=512):
    tokens, d = h.shape
    d_ff = w_down.shape[0]
    tm = min(tm, seq)
    nf = d_ff // tf
    width = w_conv.shape[0]
    assert d_ff % tf == 0 and seq % tm == 0 and width - 1 <= FFN_HALO
    kern = functools.partial(_ffn_kernel, tiles_per_seq=seq // tm, final_norm=final_norm)
    return pl.pallas_call(
        kern,
        grid=(tokens // tm, nf),
        in_specs=[
            pl.BlockSpec((tm, d), lambda i, f: (i, 0)),
            pl.BlockSpec((FFN_HALO, d), lambda i, f: (jnp.maximum(i * (tm // FFN_HALO) - 1, 0), 0)),
            pl.BlockSpec((1, d), lambda i, f: (0, 0)),
            pl.BlockSpec((d, tf), lambda i, f: (0, f)),
            pl.BlockSpec((d, tf), lambda i, f: (0, nf + f)),
            pl.BlockSpec((width, tf), lambda i, f: (0, f)),
            pl.BlockSpec((width, tf), lambda i, f: (0, nf + f)),
            pl.BlockSpec((1, tf), lambda i, f: (0, f)),
            pl.BlockSpec((1, tf), lambda i, f: (0, nf + f)),
            pl.BlockSpec((tf, d), lambda i, f: (f, 0)),
            pl.BlockSpec((1, d), lambda i, f: (0, 0)),
        ],
        out_specs=pl.BlockSpec((tm, d), lambda i, f: (i, 0)),
        out_shape=jax.ShapeDtypeStruct((tokens, d), F32),
        scratch_shapes=[pltpu.VMEM((tm + FFN_HALO, d), BF16), pltpu.VMEM((tm, d), F32)],
        compiler_params=_params("parallel", "arbitrary"),
        name="conv_ffn",
    )(h, h, g.reshape(1, d), w_up, w_up, w_conv, w_conv, b_conv.reshape(1, -1),
      b_conv.reshape(1, -1), w_down, gf.reshape(1, d))


def _dsa_pre_kernel(h_ref, g_ref, win_ref, gq_ref, gkv_ref, wuq_ref, wuk_ref, wqi_ref,
                    qlat_ref, qidx_ref, ckv_ref, kidx_ref, tail_ref):
    xn = _rms(h_ref[...], g_ref[...]).astype(BF16)
    p = _dot(xn, win_ref[...])
    cq = _rms(p[:, :Q_LORA], gq_ref[...]).astype(BF16)
    ckv_ref[...] = _rms(p[:, Q_LORA:Q_LORA + KV_LORA], gkv_ref[...]).astype(BF16)
    tail = p[:, Q_LORA + KV_LORA:]
    lane = lax.broadcasted_iota(jnp.int32, tail.shape, 1)
    kidx_ref[...] = jnp.where(lane < IDX_DIM, tail, 0.0).astype(BF16)
    tail_ref[...] = tail
    q = _dot(cq, wuq_ref[...]).astype(BF16)
    qi = _dot(cq, wqi_ref[...])
    q_scale = (ATT_HEAD_DIM ** -0.5) * LOG2E
    for hd in range(ATT_HEADS):
        sl = slice(hd * ATT_HEAD_DIM, (hd + 1) * ATT_HEAD_DIM)
        ql = (_dot(q[:, sl], wuk_ref[hd]) * q_scale).astype(BF16)
        qx = qi[:, sl].astype(BF16)
        for blk in range(DSA_PRE_BLOCKS):
            rows = slice(blk * DSA_TQ, (blk + 1) * DSA_TQ)
            qlat_ref[blk, hd] = ql[rows]
            qidx_ref[blk, hd] = qx[rows]


def _dsa_pre(h, g, w_in, g_q, g_kv, w_uq, w_uk, w_qidx):
    tokens, d = h.shape
    nq = DSA_PRE_BLOCKS
    tm = nq * DSA_TQ
    nblk = tokens // DSA_TQ
    hq = ATT_HEADS * ATT_HEAD_DIM
    return pl.pallas_call(
        _dsa_pre_kernel,
        grid=(tokens // tm,),
        in_specs=[
            pl.BlockSpec((tm, d), lambda i: (i, 0)),
            _const_spec((1, d)),
            _const_spec((d, A_IN_PAD)),
            _const_spec((1, Q_LORA)),
            _const_spec((1, KV_LORA)),
            _const_spec((Q_LORA, hq)),
            _const_spec((ATT_HEADS, ATT_HEAD_DIM, KV_LORA)),
            _const_spec((Q_LORA, IDX_HEADS * LANES)),
        ],
        out_specs=[
            pl.BlockSpec((nq, ATT_HEADS, DSA_TQ, KV_LORA), lambda i: (i, 0, 0, 0)),
            pl.BlockSpec((nq, IDX_HEADS, DSA_TQ, LANES), lambda i: (i, 0, 0, 0)),
            pl.BlockSpec((tm, KV_LORA), lambda i: (i, 0)),
            pl.BlockSpec((tm, LANES), lambda i: (i, 0)),
            pl.BlockSpec((tm, LANES), lambda i: (i, 0)),
        ],
        out_shape=[
            jax.ShapeDtypeStruct((nblk, ATT_HEADS, DSA_TQ, KV_LORA), BF16),
            jax.ShapeDtypeStruct((nblk, IDX_HEADS, DSA_TQ, LANES), BF16),
            jax.ShapeDtypeStruct((tokens, KV_LORA), BF16),
            jax.ShapeDtypeStruct((tokens, LANES), BF16),
            jax.ShapeDtypeStruct((tokens, LANES), F32),
        ],
        compiler_params=_params("parallel"),
        name="dsa_pre",
    )(h, g.reshape(1, d), w_in, g_q.reshape(1, -1), g_kv.reshape(1, -1), w_uq, w_uk, w_qidx)


def _t5_bucket_np(dist):
    max_exact = REL_BUCKETS // 2
    n = np.maximum(dist, 0)
    nf = np.maximum(n, 1).astype(np.float32)
    large = max_exact + (np.log(nf / np.float32(max_exact)) / np.float32(math.log(REL_MAX_DIST / max_exact))
                         * np.float32(REL_BUCKETS - max_exact)).astype(np.int32)
    large = np.minimum(large, REL_BUCKETS - 1)
    return np.where(n < max_exact, n, large).astype(np.int32)


def _bias_tile_buckets():
    i = np.arange(DSA_TQ)[:, None]
    j = np.arange(DSA_TQ)[None, :]
    d0 = _t5_bucket_np(i - j)
    d1 = _t5_bucket_np(DSA_TQ + i - j)
    far = _t5_bucket_np(np.full((DSA_TQ, DSA_TQ), 2 * DSA_TQ + 1))
    assert (far == _t5_bucket_np(np.full((DSA_TQ, DSA_TQ), 10 ** 6))).all()
    kinds = [np.concatenate(p, axis=1) for p in ((far, far), (d0, far), (far, d1), (d1, d0))]
    return np.stack(kinds).astype(np.int32)


def _far_bucket():
    return int(_t5_bucket_np(np.array(2 * DSA_TQ + 1)))


def _bias_expand_kernel(idx_ref, rb_ref, o_ref):
    hd = pl.program_id(1)
    idx = idx_ref[0]
    out = jnp.zeros(idx.shape, F32)
    for b in range(REL_BUCKETS):
        out = jnp.where(idx == b, rb_ref[b, hd], out)
    o_ref[0, 0] = (out - rb_ref[_far_bucket(), hd]) * LOG2E


def _bias_tiles(rel_bias):
    idx = jnp.asarray(_bias_tile_buckets())
    return pl.pallas_call(
        _bias_expand_kernel,
        grid=(4, ATT_HEADS),
        in_specs=[
            pl.BlockSpec((1, DSA_TQ, DSA_TK), lambda k, hd: (k, 0, 0)),
            pl.BlockSpec(memory_space=pltpu.SMEM),
        ],
        out_specs=pl.BlockSpec((1, 1, DSA_TQ, DSA_TK), lambda k, hd: (k, hd, 0, 0)),
        out_shape=jax.ShapeDtypeStruct((4, ATT_HEADS, DSA_TQ, DSA_TK), F32),
        name="dsa_bias_tiles",
    )(idx, rel_bias)


def _sortable(x):
    bits = lax.bitcast_convert_type(x, jnp.int32)
    return jnp.where(bits < 0, bits ^ jnp.int32(0x7FFFFFFF), bits)


def _dsa_attn_kernel(qlat_ref, qidx_ref, tail_ref, ckv_ref, kidx_ref, bias_ref, o_ref,
                     keys_ref, keyst_ref, wb_ref, m_ref, l_ref, alpha_ref, acc_ref, lg_ref, p_ref,
                     *, topk):
    qb = pl.program_id(1)
    tq, tk = DSA_TQ, DSA_TK
    rows = ATT_HEADS * tq
    n_kt = qb // (tk // tq) + 1
    q_pos = qb * tq + lax.broadcasted_iota(jnp.int32, (tq, tk), 0)
    k_off = lax.broadcasted_iota(jnp.int32, (tq, tk), 1)

    tail = tail_ref[...]
    w_scale = (IDX_HEADS ** -0.5) * (IDX_DIM ** -0.5)
    for hd in range(IDX_HEADS):
        col = tail[:, IDX_DIM + hd:IDX_DIM + hd + 1] * w_scale
        wb_ref[hd * tq:(hd + 1) * tq, :] = jnp.broadcast_to(col, (tq, LANES))

    def score_pair(pair, carry):
        for t in range(2):
            kt = 2 * pair + t
            start = pl.multiple_of(kt * tk, tk)
            kk = kidx_ref[pl.ds(start, tk), :]
            logits = jnp.maximum(_dot_nt(qidx_ref[0], kk), 0.0)
            wb = wb_ref[...]
            weighted = jnp.concatenate(
                [logits[:, c * LANES:(c + 1) * LANES] * wb for c in range(tk // LANES)], axis=1)
            score = jnp.sum(weighted.reshape(IDX_HEADS, tq, tk), axis=0)
            score = jnp.where(start + k_off <= q_pos, score, -jnp.inf)
            keys_ref[kt] = _sortable(score)
            keyst_ref[kt] = _sortable(score.T)
        return carry

    lax.fori_loop(0, (n_kt + 1) // 2, score_pair, 0)

    reps = tk // LANES

    def count_ge(cand):
        def body(pair, cnt):
            ge = (keyst_ref[pl.ds(2 * pair, 2)] >= cand).astype(jnp.int32)
            return cnt + jnp.sum(ge.reshape(2 * tk // SUBLANES, SUBLANES, tq), axis=0)
        cnt = lax.fori_loop(0, (n_kt + 1) // 2, body, jnp.zeros((SUBLANES, tq), jnp.int32))
        return jnp.sum(cnt, axis=0, keepdims=True)

    zero = jnp.zeros((1, tq), jnp.int32)
    thr0 = jnp.where(count_ge(zero) >= topk, zero, jnp.full((1, tq), INT_MIN, jnp.int32))

    def bit_step(it, t):
        cand = t | jnp.left_shift(jnp.int32(1), 30 - it)
        return jnp.where(count_ge(cand) >= topk, cand, t)

    thr_row = lax.fori_loop(0, 31, bit_step, thr0)
    thr = lax.bitcast_convert_type(
        jnp.broadcast_to(lax.bitcast_convert_type(thr_row, F32), (LANES, tq)).T, jnp.int32)

    m_ref[...] = jnp.full(m_ref.shape, MASKED, F32)
    l_ref[...] = jnp.zeros_like(l_ref)
    acc_ref[...] = jnp.zeros_like(acc_ref)
    parity = qb % 2

    def head_update(hd):
        r = slice(hd * tq, (hd + 1) * tq)
        lg = lg_ref[r, :]
        m_prev = m_ref[r, :]
        tile_max = jnp.maximum(lg[:, :LANES], lg[:, LANES:2 * LANES])
        for c in range(2, reps):
            tile_max = jnp.maximum(tile_max, lg[:, c * LANES:(c + 1) * LANES])
        m_new = jnp.maximum(m_prev, jnp.max(tile_max, axis=-1, keepdims=True))
        alpha = jnp.exp2(m_prev - m_new)
        p = jnp.exp2(lg - jnp.concatenate([m_new] * reps, axis=1))
        tile_sum = p[:, :LANES] + p[:, LANES:2 * LANES]
        for c in range(2, reps):
            tile_sum = tile_sum + p[:, c * LANES:(c + 1) * LANES]
        l_ref[r, :] = alpha * l_ref[r, :] + jnp.sum(tile_sum, axis=-1, keepdims=True)
        m_ref[r, :] = m_new
        alpha_ref[r, :] = alpha
        p_ref[r, :] = p.astype(BF16)

    def attend(kt, near):
        start = pl.multiple_of(kt * tk, tk)
        kv = ckv_ref[pl.ds(start, tk), :]
        ks = keys_ref[kt]
        sel = jnp.concatenate(
            [ks[:, c * LANES:(c + 1) * LANES] >= thr for c in range(reps)], axis=1)
        if near:
            sel = jnp.logical_and(sel, start + k_off <= q_pos)
        shift = jnp.where(sel, 0.0, MASKED)[None]
        if near:
            kind = jnp.where(kt == n_kt - 1, 1 + 2 * parity, 2 * (1 - parity))
            shift = shift + bias_ref[kind]
        logits = _dot_nt(qlat_ref[0], kv).reshape(ATT_HEADS, tq, tk) + shift
        lg_ref[...] = logits.reshape(rows, tk)
        for hd in range(ATT_HEADS):
            head_update(hd)
        alpha = alpha_ref[...]
        acc_ref[...] = (acc_ref[...] * jnp.concatenate([alpha] * (KV_LORA // LANES), axis=1)
                        + _dot(p_ref[...], kv))

    n_far = jnp.maximum(n_kt - 2, 0)

    def far_tile(kt, carry):
        attend(kt, False)
        return carry

    def near_tile(kt, carry):
        attend(kt, True)
        return carry

    lax.fori_loop(0, n_far, far_tile, 0)
    lax.fori_loop(n_far, n_kt, near_tile, 0)

    inv = 1.0 / l_ref[...]
    out = acc_ref[...] * jnp.concatenate([inv] * (KV_LORA // LANES), axis=1)
    for hd in range(ATT_HEADS):
        o_ref[:, hd * KV_LORA:(hd + 1) * KV_LORA] = out[hd * tq:(hd + 1) * tq, :].astype(BF16)


def _dsa_attn(qlat, qidx, tail, ckv, kidx, bias, *, batch, seq):
    tq, tk = DSA_TQ, DSA_TK
    assert seq % tk == 0 and tq == LANES and tk == 2 * tq
    nqb = seq // tq
    rows = ATT_HEADS * tq
    topk = min(TOPK_MAX, seq // 4)
    tokens = batch * seq
    qlat = qlat.reshape(batch * nqb, rows, KV_LORA)
    qidx = qidx.reshape(batch * nqb, rows, LANES)
    kern = functools.partial(_dsa_attn_kernel, topk=topk)
    return pl.pallas_call(
        kern,
        grid=(batch, nqb),
        in_specs=[
            pl.BlockSpec((1, rows, KV_LORA), lambda b, q: (b * nqb + q, 0, 0)),
            pl.BlockSpec((1, rows, LANES), lambda b, q: (b * nqb + q, 0, 0)),
            pl.BlockSpec((tq, LANES), lambda b, q: (b * nqb + q, 0)),
            pl.BlockSpec((seq, KV_LORA), lambda b, q: (b, 0)),
            pl.BlockSpec((seq, LANES), lambda b, q: (b, 0)),
            _const_spec((4, ATT_HEADS, tq, tk)),
        ],
        out_specs=pl.BlockSpec((tq, ATT_HEADS * KV_LORA), lambda b, q: (b * nqb + q, 0)),
        out_shape=jax.ShapeDtypeStruct((tokens, ATT_HEADS * KV_LORA), BF16),
        scratch_shapes=[
            pltpu.VMEM((seq // tk, tq, tk), jnp.int32),
            pltpu.VMEM((seq // tk, tk, tq), jnp.int32),
            pltpu.VMEM((rows, LANES), F32),
            pltpu.VMEM((rows, LANES), F32),
            pltpu.VMEM((rows, LANES), F32),
            pltpu.VMEM((rows, LANES), F32),
            pltpu.VMEM((rows, KV_LORA), F32),
            pltpu.VMEM((rows, tk), F32),
            pltpu.VMEM((rows, tk), BF16),
        ],
        compiler_params=_params("parallel", "arbitrary"),
        name="dsa_attn",
    )(qlat, qidx, tail, ckv, kidx, bias)


def _dsa_post_kernel(olat_ref, h_ref, wuv_ref, wo_ref, o_ref, heads_ref):
    for hd in range(ATT_HEADS):
        oh = _dot(olat_ref[:, hd * KV_LORA:(hd + 1) * KV_LORA], wuv_ref[hd])
        heads_ref[:, hd * ATT_HEAD_DIM:(hd + 1) * ATT_HEAD_DIM] = oh.astype(BF16)
    o_ref[...] = h_ref[...] + _dot(heads_ref[...], wo_ref[...])


def _dsa_post(olat, h, w_uv, w_o, tm=256):
    tokens, d = h.shape
    hq = ATT_HEADS * ATT_HEAD_DIM
    return pl.pallas_call(
        _dsa_post_kernel,
        grid=(tokens // tm,),
        in_specs=[
            pl.BlockSpec((tm, ATT_HEADS * KV_LORA), lambda i: (i, 0)),
            pl.BlockSpec((tm, d), lambda i: (i, 0)),
            _const_spec((ATT_HEADS, KV_LORA, ATT_HEAD_DIM)),
            _const_spec((hq, d)),
        ],
        out_specs=pl.BlockSpec((tm, d), lambda i: (i, 0)),
        out_shape=jax.ShapeDtypeStruct((tokens, d), F32),
        scratch_shapes=[pltpu.VMEM((tm, hq), BF16)],
        compiler_params=_params("parallel"),
        name="dsa_post",
    )(olat, h, w_uv, w_o)


def _dsa_layer(h, g, rel_bias, w_in, g_q, g_kv, w_uq, w_qidx, w_uk, w_uv, w_o, *, batch, seq):
    d = h.shape[1]
    w_in_p = jnp.pad(w_in, ((0, 0), (0, A_IN_PAD - w_in.shape[1]))).astype(BF16)
    w_qidx_p = jnp.pad(w_qidx.reshape(Q_LORA, IDX_HEADS, IDX_DIM),
                       ((0, 0), (0, 0), (0, LANES - IDX_DIM))).reshape(Q_LORA, IDX_HEADS * LANES)
    qlat, qidx, ckv, kidx, tail = _dsa_pre(
        h, g, w_in_p, g_q, g_kv, w_uq.astype(BF16), w_uk.astype(BF16), w_qidx_p.astype(BF16))
    bias = _bias_tiles(rel_bias)
    olat = _dsa_attn(qlat, qidx, tail, ckv, kidx, bias, batch=batch, seq=seq)
    return _dsa_post(olat, h, w_uv.astype(BF16), w_o.astype(BF16))


def _hg_proj_kernel(h_ref, g_ref, w_ref, lb_ref, o_ref, xn_ref, *, blocks_per_part):
    j = pl.program_id(1)

    @pl.when(j == 0)
    def _():
        xn_ref[...] = _rms(h_ref[...], g_ref[...]).astype(BF16)

    y = _dot(xn_ref[...], w_ref[...])
    part = j // blocks_per_part

    @pl.when(jnp.logical_or(part == 0, part == 3))
    def _():
        o_ref[...] = _silu(y)

    @pl.when(part == 1)
    def _():
        lb = lb_ref[...]
        o_ref[...] = lb + (1.0 - lb) * _sigmoid(y)

    @pl.when(part == 2)
    def _():
        o_ref[...] = y


def _hg_proj(h, g, w_in, lb, tm=512, tn=2048):
    tokens, d = h.shape
    n = w_in.shape[1]
    bpp = d // tn
    kern = functools.partial(_hg_proj_kernel, blocks_per_part=bpp)
    return pl.pallas_call(
        kern,
        grid=(tokens // tm, n // tn),
        in_specs=[
            pl.BlockSpec((tm, d), lambda i, j: (i, 0)),
            pl.BlockSpec((1, d), lambda i, j: (0, 0)),
            pl.BlockSpec((d, tn), lambda i, j: (0, j)),
            pl.BlockSpec((1, tn), lambda i, j: (0, j % bpp)),
        ],
        out_specs=pl.BlockSpec((tm, tn), lambda i, j: (i, j)),
        out_shape=jax.ShapeDtypeStruct((tokens, n), F32),
        scratch_shapes=[pltpu.VMEM((tm, d), BF16)],
        compiler_params=_params("parallel", "arbitrary"),
        name="hgrn2_proj",
    )(h, g.reshape(1, d), w_in, lb.reshape(1, d))


def _hg_rec_kernel(q_ref, f_ref, v_ref, tri_ref, o_ref, a_ref, state_ref):
    tt = pl.program_id(1)
    rows, width = q_ref.shape
    dk = width // HG_HEADS
    sub = HG_SUB

    @pl.when(tt == 0)
    def _():
        state_ref[...] = jnp.zeros_like(state_ref)

    log_f = jnp.log(f_ref[...])
    tri = tri_ref[...]
    a_sum = None
    for _ in range(3):
        part = log_f.astype(BF16)
        log_f = log_f - part.astype(F32)
        term = _dot(tri, part)
        a_sum = term if a_sum is None else a_sum + term
    a_ref[...] = a_sum

    def step(j, carry):
        r0 = pl.multiple_of(j * sub, sub)
        groups = sub // SUBLANES
        for hd in range(HG_HEADS):
            sl = slice(hd * dk, (hd + 1) * dk)
            a = a_ref[pl.ds(r0, sub), sl]
            q = q_ref[pl.ds(r0, sub), sl]
            k = 1.0 - f_ref[pl.ds(r0, sub), sl]
            v = v_ref[pl.ds(r0, sub), sl]
            a_last = a[sub - 1:sub, :]

            intra = [jnp.zeros((SUBLANES, dk), F32) for _ in range(groups)]
            for s in range(sub):
                g0 = s // SUBLANES
                lo = g0 * SUBLANES
                t_idx = lo + lax.broadcasted_iota(jnp.int32, (sub - lo, 1), 0)
                e = jnp.exp(jnp.where(t_idx >= s, a[lo:, :] - a[s:s + 1, :], -jnp.inf))
                w = q[lo:, :] * (k[s:s + 1, :] * e)
                upd = jnp.sum(w, axis=-1, keepdims=True) * v[s:s + 1, :]
                for g in range(g0, groups):
                    rows_g = slice((g - g0) * SUBLANES, (g - g0 + 1) * SUBLANES)
                    intra[g] = intra[g] + upd[rows_g, :]

            st = state_ref[hd]
            q_dec = (q * jnp.exp(a)).astype(BF16)
            k_dec = (k * jnp.exp(a_last - a)).astype(BF16)
            inter = _dot_nt(q_dec, st.astype(BF16))
            o_ref[pl.ds(r0, sub), sl] = inter + jnp.concatenate(intra, axis=0)
            state_ref[hd] = st * jnp.exp(a_last) + _dot_tn(v.astype(BF16), k_dec)
        return carry

    lax.fori_loop(0, rows // sub, step, 0)


def _hg_rec(proj, *, batch, seq, d, tt=256):
    tokens = batch * seq
    tt = min(tt, seq)
    nt = seq // tt
    dk = d // HG_HEADS
    tri = (np.arange(tt)[:, None] >= np.arange(tt)[None, :]) & (
        np.arange(tt)[:, None] // HG_SUB == np.arange(tt)[None, :] // HG_SUB)
    tri = jnp.asarray(tri.astype(np.float32)).astype(BF16)
    return pl.pallas_call(
        _hg_rec_kernel,
        grid=(batch, nt),
        in_specs=[
            pl.BlockSpec((tt, d), lambda b, t: (b * nt + t, 0)),
            pl.BlockSpec((tt, d), lambda b, t: (b * nt + t, 1)),
            pl.BlockSpec((tt, d), lambda b, t: (b * nt + t, 2)),
            _const_spec((tt, tt)),
        ],
        out_specs=pl.BlockSpec((tt, d), lambda b, t: (b * nt + t, 0)),
        out_shape=jax.ShapeDtypeStruct((tokens, d), F32),
        scratch_shapes=[pltpu.VMEM((tt, d), F32), pltpu.VMEM((HG_HEADS, dk, dk), F32)],
        compiler_params=_params("parallel", "arbitrary"),
        name="hgrn2_recurrence",
    )(proj, proj, proj, tri)


def _hg_post_kernel(o_ref, gate_ref, h_ref, gn_ref, w_ref, out_ref, y_ref):
    width = o_ref.shape[1]
    dv = width // HG_HEADS
    for hd in range(HG_HEADS):
        sl = slice(hd * dv, (hd + 1) * dv)
        o = o_ref[:, sl]
        o = o * lax.rsqrt(jnp.mean(o * o, axis=-1, keepdims=True) + EPS)
        y_ref[:, sl] = (o * gn_ref[:, sl] * gate_ref[:, sl]).astype(BF16)
    out_ref[...] = h_ref[...] + _dot(y_ref[...], w_ref[...])


def _hg_post(o, proj, h, g_norm, w_o, tm=256):
    tokens, d = h.shape
    return pl.pallas_call(
        _hg_post_kernel,
        grid=(tokens // tm,),
        in_specs=[
            pl.BlockSpec((tm, d), lambda i: (i, 0)),
            pl.BlockSpec((tm, d), lambda i: (i, 3)),
            pl.BlockSpec((tm, d), lambda i: (i, 0)),
            _const_spec((1, d)),
            _const_spec((d, d)),
        ],
        out_specs=pl.BlockSpec((tm, d), lambda i: (i, 0)),
        out_shape=jax.ShapeDtypeStruct((tokens, d), F32),
        scratch_shapes=[pltpu.VMEM((tm, d), BF16)],
        compiler_params=_params("parallel"),
        name="hgrn2_post",
    )(o, proj, h, g_norm.reshape(1, d), w_o)


def _hgrn2_layer(h, g, w_in, lb, g_norm, w_o, *, batch, seq):
    d = h.shape[1]
    proj = _hg_proj(h, g, w_in.astype(BF16), lb)
    o = _hg_rec(proj, batch=batch, seq=seq, d=d)
    return _hg_post(o, proj, h, g_norm, w_o.astype(BF16))


def _lower_bounds_kernel(b_ref, o_ref):
    b = b_ref[...]
    e = jnp.exp(b - jnp.max(b, axis=0, keepdims=True))
    soft = e / jnp.sum(e, axis=0, keepdims=True)
    run = jnp.zeros_like(soft[0:1])
    for layer in range(soft.shape[0]):
        if layer > 0:
            run = run + soft[layer:layer + 1]
        o_ref[layer:layer + 1, :] = run


def _lower_bounds(b):
    return pl.pallas_call(
        _lower_bounds_kernel,
        out_shape=jax.ShapeDtypeStruct(b.shape, F32),
        name="hgrn2_lower_bounds",
    )(b)


POOL_HALO = 16


def _pool_kernel(h_ref, halo_ref, g_ref, w_ref, sc_ref, o_ref, *, tiles_per_seq):
    i = pl.program_id(0)
    tm, d = h_ref.shape
    group = d // len(POOL_WINDOWS)
    g = g_ref[...]
    u = _rms(h_ref[...], g)
    hu = _rms(halo_ref[...], g) * _not_first(i, tiles_per_seq)
    ext = jnp.concatenate([hu, u], axis=0)
    pos = (i % tiles_per_seq) * tm + lax.broadcasted_iota(jnp.int32, (tm, group), 0)
    for gi, win in enumerate(POOL_WINDOWS):
        sl = slice(gi * group, (gi + 1) * group)
        s = ext[:, sl]
        span = 1
        while span < win:
            s = s + pltpu.roll(s, span, axis=0)
            span *= 2
        count = jnp.minimum(pos + 1, win).astype(F32)
        diff = s[POOL_HALO:, :] / count - u[:, sl]
        y = _dot(diff.astype(BF16), w_ref[gi])
        o_ref[:, sl] = h_ref[:, sl] + y * sc_ref[:, sl]


def _pool_layer(h, g, w_group, scale, *, seq, tm=256):
    tokens, d = h.shape
    tm = min(tm, seq)
    assert max(POOL_WINDOWS) <= POOL_HALO
    kern = functools.partial(_pool_kernel, tiles_per_seq=seq // tm)
    return pl.pallas_call(
        kern,
        grid=(tokens // tm,),
        in_specs=[
            pl.BlockSpec((tm, d), lambda i: (i, 0)),
            pl.BlockSpec((POOL_HALO, d), lambda i: (jnp.maximum(i * (tm // POOL_HALO) - 1, 0), 0)),
            _const_spec((1, d)),
            _const_spec(w_group.shape),
            _const_spec((1, d)),
        ],
        out_specs=pl.BlockSpec((tm, d), lambda i: (i, 0)),
        out_shape=jax.ShapeDtypeStruct((tokens, d), F32),
        compiler_params=_params("parallel"),
        name="pool_mixer",
    )(h, h, g.reshape(1, d), w_group.astype(BF16), scale.reshape(1, d))


CONF_HALO = 32


def _conf_glu_kernel(h_ref, g_ref, wa_ref, wg_ref, ba_ref, bg_ref, o_ref, xn_ref):
    @pl.when(pl.program_id(1) == 0)
    def _():
        xn_ref[...] = _rms(h_ref[...], g_ref[...]).astype(BF16)

    xn = xn_ref[...]
    a = _dot(xn, wa_ref[...]) + ba_ref[...]
    gate = _dot(xn, wg_ref[...]) + bg_ref[...]
    o_ref[...] = a * _sigmoid(gate)


def _conf_glu(h, g, w_pw1, b_pw1, tm=512, tn=1024):
    tokens, d = h.shape
    nj = d // tn
    b = b_pw1.reshape(1, -1)
    return pl.pallas_call(
        _conf_glu_kernel,
        grid=(tokens // tm, nj),
        in_specs=[
            pl.BlockSpec((tm, d), lambda i, j: (i, 0)),
            pl.BlockSpec((1, d), lambda i, j: (0, 0)),
            pl.BlockSpec((d, tn), lambda i, j: (0, j)),
            pl.BlockSpec((d, tn), lambda i, j: (0, nj + j)),
            pl.BlockSpec((1, tn), lambda i, j: (0, j)),
            pl.BlockSpec((1, tn), lambda i, j: (0, nj + j)),
        ],
        out_specs=pl.BlockSpec((tm, tn), lambda i, j: (i, j)),
        out_shape=jax.ShapeDtypeStruct((tokens, d), F32),
        scratch_shapes=[pltpu.VMEM((tm, d), BF16)],
        compiler_params=_params("parallel", "arbitrary"),
        name="conformer_glu",
    )(h, g.reshape(1, d), w_pw1, w_pw1, b, b)


def _conf_conv_kernel(u_ref, halo_ref, h_ref, wdw_ref, bdw_ref, lng_ref, lnb_ref, w2_ref, b2_ref,
                      o_ref, ext_ref, c_ref, *, tiles_per_seq):
    i = pl.program_id(0)
    tm, d = u_ref.shape
    width = wdw_ref.shape[0]
    ext_ref[:CONF_HALO, :] = halo_ref[...] * _not_first(i, tiles_per_seq)
    ext_ref[CONF_HALO:, :] = u_ref[...]
    base = CONF_HALO - (width - 1)
    n_ext = tm + CONF_HALO
    for c in range(d // LANES):
        sl = slice(c * LANES, (c + 1) * LANES)
        ext = ext_ref[:, sl]
        acc = jnp.broadcast_to(bdw_ref[:, sl], (tm, LANES))
        for phase in range(SUBLANES):
            taps = [j for j in range(width) if (base + j) % SUBLANES == phase]
            if not taps:
                continue
            shifted = ext if phase == 0 else pltpu.roll(ext, n_ext - phase, axis=0)
            for j in taps:
                off = base + j - phase
                acc = acc + wdw_ref[j:j + 1, sl] * shifted[off:off + tm, :]
        c_ref[:, sl] = acc
    x = c_ref[...]
    mu = jnp.mean(x, axis=-1, keepdims=True)
    xc = x - mu
    y = xc * lax.rsqrt(jnp.mean(xc * xc, axis=-1, keepdims=True) + EPS)
    y = _silu(y * lng_ref[...] + lnb_ref[...]).astype(BF16)
    o_ref[...] = h_ref[...] + _dot(y, w2_ref[...]) + b2_ref[...]


def _conf_conv(u, h, w_dw, b_dw, ln_g, ln_b, w_pw2, b_pw2, *, seq, tm=256):
    tokens, d = h.shape
    tm = min(tm, seq)
    width = w_dw.shape[0]
    assert width - 1 <= CONF_HALO
    kern = functools.partial(_conf_conv_kernel, tiles_per_seq=seq // tm)
    return pl.pallas_call(
        kern,
        grid=(tokens // tm,),
        in_specs=[
            pl.BlockSpec((tm, d), lambda i: (i, 0)),
            pl.BlockSpec((CONF_HALO, d), lambda i: (jnp.maximum(i * (tm // CONF_HALO) - 1, 0), 0)),
            pl.BlockSpec((tm, d), lambda i: (i, 0)),
            _const_spec((width, d)),
            _const_spec((1, d)),
            _const_spec((1, d)),
            _const_spec((1, d)),
            _const_spec((d, d)),
            _const_spec((1, d)),
        ],
        out_specs=pl.BlockSpec((tm, d), lambda i: (i, 0)),
        out_shape=jax.ShapeDtypeStruct((tokens, d), F32),
        scratch_shapes=[pltpu.VMEM((tm + CONF_HALO, d), F32), pltpu.VMEM((tm, d), F32)],
        compiler_params=_params("parallel"),
        name="conformer_conv",
    )(u, u, h, w_dw, b_dw.reshape(1, d), ln_g.reshape(1, d), ln_b.reshape(1, d), w_pw2,
      b_pw2.reshape(1, d))


def _conformer_layer(h, g, w_pw1, b_pw1, w_dw, b_dw, ln_g, ln_b, w_pw2, b_pw2, *, seq):
    u = _conf_glu(h, g, w_pw1.astype(BF16), b_pw1)
    return _conf_conv(u, h, w_dw, b_dw, ln_g, ln_b, w_pw2.astype(BF16), b_pw2, seq=seq)


def kernel(x, rel_bias, a_w_in, a_g_q, a_g_kv, a_w_uq, a_w_qidx, a_w_uk, a_w_uv, a_w_o, b_w_in, b_lower_bounds, b_g_norm, b_w_o, c_w_group, c_scale, d_w_pw1, d_b_pw1, d_w_dw, d_b_dw, d_ln_g, d_ln_b, d_w_pw2, d_b_pw2, norm_mix, norm_ffn, ffn_w_up, ffn_w_conv, ffn_b_conv, ffn_w_down, final_norm):
    batch, seq, d = x.shape
    depth = norm_mix.shape[0]
    lb_all = _lower_bounds(b_lower_bounds)
    h = x.reshape(batch * seq, d)
    for i in range(depth):
        j, kind = divmod(i, 4)
        g = norm_mix[i]
        if kind == 0:
            h = _dsa_layer(h, g, rel_bias, a_w_in[j], a_g_q[j], a_g_kv[j], a_w_uq[j], a_w_qidx[j],
                           a_w_uk[j], a_w_uv[j], a_w_o[j], batch=batch, seq=seq)
        elif kind == 1:
            h = _hgrn2_layer(h, g, b_w_in[j], lb_all[i], b_g_norm[j], b_w_o[j], batch=batch, seq=seq)
        elif kind == 2:
            h = _pool_layer(h, g, c_w_group[j], c_scale[j], seq=seq)
        else:
            h = _conformer_layer(h, g, d_w_pw1[j], d_b_pw1[j], d_w_dw[j], d_b_dw[j], d_ln_g[j],
                                 d_ln_b[j], d_w_pw2[j], d_b_pw2[j], seq=seq)
        h = _ffn_layer(h, norm_ffn[i], ffn_w_up[i].astype(BF16), ffn_w_conv[i], ffn_b_conv[i],
                       ffn_w_down[i].astype(BF16), final_norm, seq=seq,
                       final_norm=(i == depth - 1))
    return h.reshape(batch, seq, d)
```

```python
import functools
import math

import jax
import jax.numpy as jnp
import numpy as np
from jax import lax
from jax.experimental import pallas as pl
from jax.experimental.pallas import tpu as pltpu

F32 = jnp.float32
BF16 = jnp.bfloat16

EPS = 1e-6
ATT_HEADS = 16
ATT_HEAD_DIM = 128
Q_LORA = 512
KV_LORA = 256
IDX_HEADS = 16
IDX_DIM = 64
TOPK_MAX = 256
REL_BUCKETS = 32
REL_MAX_DIST = 128
HG_HEADS = 16
POOL_WINDOWS = (2, 4, 8, 16)

LANES = 128
SUBLANES = 8
VMEM_LIMIT = 56 * 1024 * 1024
MASKED = -1e30
INT_MIN = -2 ** 31

A_IN_PAD = 896
DSA_TQ = 128
DSA_TK = 256
DSA_PRE_BLOCKS = 2
HG_SUB = 16
LOG2E = math.log2(math.e)


def _dot(a, b):
    return jnp.dot(a, b, preferred_element_type=F32)


def _dot_nt(a, b):
    return lax.dot_general(a, b, (((1,), (1,)), ((), ())), preferred_element_type=F32)


def _dot_tn(a, b):
    return lax.dot_general(a, b, (((0,), (0,)), ((), ())), preferred_element_type=F32)


def _rms(x, g):
    return x * lax.rsqrt(jnp.mean(x * x, axis=-1, keepdims=True) + EPS) * g


def _sigmoid(x):
    return 1.0 / (1.0 + jnp.exp(-x))


def _silu(x):
    return x * _sigmoid(x)


def _not_first(tile, tiles_per_seq):
    return jnp.where(tile % tiles_per_seq != 0, 1.0, 0.0).astype(F32)


def _params(*sem):
    return pltpu.CompilerParams(dimension_semantics=sem, vmem_limit_bytes=VMEM_LIMIT)


def _const_spec(shape):
    nd = len(shape)
    return pl.BlockSpec(shape, lambda *_: (0,) * nd)


FFN_HALO = 16


def _ffn_kernel(h_ref, halo_ref, g_ref, wa_ref, wb_ref, ca_ref, cb_ref, ba_ref, bb_ref, wd_ref,
                gf_ref, o_ref, xn_ref, acc_ref, *, tiles_per_seq, final_norm):
    i = pl.program_id(0)
    f = pl.program_id(1)
    tm = h_ref.shape[0]

    @pl.when(f == 0)
    def _():
        g = g_ref[...]
        xn_ref[FFN_HALO:, :] = _rms(h_ref[...], g).astype(BF16)
        hn = _rms(halo_ref[...], g) * _not_first(i, tiles_per_seq)
        xn_ref[:FFN_HALO, :] = hn.astype(BF16)
        acc_ref[...] = jnp.zeros_like(acc_ref)

    xn = xn_ref[...]
    width = ca_ref.shape[0]

    def conv(u, w_ref, b_ref):
        w = w_ref[...]
        out = b_ref[...] + w[width - 1:width, :] * u[FFN_HALO:, :]
        for j in range(width - 1):
            shift = width - 1 - j
            out = out + w[j:j + 1, :] * pltpu.roll(u, shift, axis=0)[FFN_HALO:, :]
        return out

    a = conv(_dot(xn, wa_ref[...]), ca_ref, ba_ref)
    b = conv(_dot(xn, wb_ref[...]), cb_ref, bb_ref)
    gated = (_silu(a) * b).astype(BF16)
    acc_ref[...] += _dot(gated, wd_ref[...])

    @pl.when(f == pl.num_programs(1) - 1)
    def _():
        out = h_ref[...] + acc_ref[...]
        if final_norm:
            out = _rms(out, gf_ref[...])
        o_ref[...] = out


def _ffn_layer(h, g, w_up, w_conv, b_conv, w_down, gf, *, seq, final_norm, tm=512, tf=512):
    tokens, d = h.shape
    d_ff = w_down.shape[0]
    tm = min(tm, seq)
    nf = d_ff // tf
    width = w_conv.shape[0]
    assert d_ff % tf == 0 and seq % tm == 0 and width - 1 <= FFN_HALO
    kern = functools.partial(_ffn_kernel, tiles_per_seq=seq // tm, final_norm=final_norm)
    return pl.pallas_call(
        kern,
        grid=(tokens // tm, nf),
        in_specs=[
            pl.BlockSpec((tm, d), lambda i, f: (i, 0)),
            pl.BlockSpec((FFN_HALO, d), lambda i, f: (jnp.maximum(i * (tm // FFN_HALO) - 1, 0), 0)),
            pl.BlockSpec((1, d), lambda i, f: (0, 0)),
            pl.BlockSpec((d, tf), lambda i, f: (0, f)),
            pl.BlockSpec((d, tf), lambda i, f: (0, nf + f)),
            pl.BlockSpec((width, tf), lambda i, f: (0, f)),
            pl.BlockSpec((width, tf), lambda i, f: (0, nf + f)),
            pl.BlockSpec((1, tf), lambda i, f: (0, f)),
            pl.BlockSpec((1, tf), lambda i, f: (0, nf + f)),
            pl.BlockSpec((tf, d), lambda i, f: (f, 0)),
            pl.BlockSpec((1, d), lambda i, f: (0, 0)),
        ],
        out_specs=pl.BlockSpec((tm, d), lambda i, f: (i, 0)),
        out_shape=jax.ShapeDtypeStruct((tokens, d), F32),
        scratch_shapes=[pltpu.VMEM((tm + FFN_HALO, d), BF16), pltpu.VMEM((tm, d), F32)],
        compiler_params=_params("parallel", "arbitrary"),
        name="conv_ffn",
    )(h, h, g.reshape(1, d), w_up, w_up, w_conv, w_conv, b_conv.reshape(1, -1),
      b_conv.reshape(1, -1), w_down, gf.reshape(1, d))


def _dsa_pre_kernel(h_ref, g_ref, win_ref, gq_ref, gkv_ref, wuq_ref, wuk_ref, wqi_ref,
                    qlat_ref, qidx_ref, ckv_ref, kidx_ref, tail_ref):
    xn = _rms(h_ref[...], g_ref[...]).astype(BF16)
    p = _dot(xn, win_ref[...])
    cq = _rms(p[:, :Q_LORA], gq_ref[...]).astype(BF16)
    ckv_ref[...] = _rms(p[:, Q_LORA:Q_LORA + KV_LORA], gkv_ref[...]).astype(BF16)
    tail = p[:, Q_LORA + KV_LORA:]
    lane = lax.broadcasted_iota(jnp.int32, tail.shape, 1)
    kidx_ref[...] = jnp.where(lane < IDX_DIM, tail, 0.0).astype(BF16)
    tail_ref[...] = tail
    q = _dot(cq, wuq_ref[...]).astype(BF16)
    qi = _dot(cq, wqi_ref[...])
    q_scale = (ATT_HEAD_DIM ** -0.5) * LOG2E
    for hd in range(ATT_HEADS):
        sl = slice(hd * ATT_HEAD_DIM, (hd + 1) * ATT_HEAD_DIM)
        ql = (_dot(q[:, sl], wuk_ref[hd]) * q_scale).astype(BF16)
        qx = qi[:, sl].astype(BF16)
        for blk in range(DSA_PRE_BLOCKS):
            rows = slice(blk * DSA_TQ, (blk + 1) * DSA_TQ)
            qlat_ref[blk, hd] = ql[rows]
            qidx_ref[blk, hd] = qx[rows]


def _dsa_pre(h, g, w_in, g_q, g_kv, w_uq, w_uk, w_qidx):
    tokens, d = h.shape
    nq = DSA_PRE_BLOCKS
    tm = nq * DSA_TQ
    nblk = tokens // DSA_TQ
    hq = ATT_HEADS * ATT_HEAD_DIM
    return pl.pallas_call(
        _dsa_pre_kernel,
        grid=(tokens // tm,),
        in_specs=[
            pl.BlockSpec((tm, d), lambda i: (i, 0)),
            _const_spec((1, d)),
            _const_spec((d, A_IN_PAD)),
            _const_spec((1, Q_LORA)),
            _const_spec((1, KV_LORA)),
            _const_spec((Q_LORA, hq)),
            _const_spec((ATT_HEADS, ATT_HEAD_DIM, KV_LORA)),
            _const_spec((Q_LORA, IDX_HEADS * LANES)),
        ],
        out_specs=[
            pl.BlockSpec((nq, ATT_HEADS, DSA_TQ, KV_LORA), lambda i: (i, 0, 0, 0)),
            pl.BlockSpec((nq, IDX_HEADS, DSA_TQ, LANES), lambda i: (i, 0, 0, 0)),
            pl.BlockSpec((tm, KV_LORA), lambda i: (i, 0)),
            pl.BlockSpec((tm, LANES), lambda i: (i, 0)),
            pl.BlockSpec((tm, LANES), lambda i: (i, 0)),
        ],
        out_shape=[
            jax.ShapeDtypeStruct((nblk, ATT_HEADS, DSA_TQ, KV_LORA), BF16),
            jax.ShapeDtypeStruct((nblk, IDX_HEADS, DSA_TQ, LANES), BF16),
            jax.ShapeDtypeStruct((tokens, KV_LORA), BF16),
            jax.ShapeDtypeStruct((tokens, LANES), BF16),
            jax.ShapeDtypeStruct((tokens, LANES), F32),
        ],
        compiler_params=_params("parallel"),
        name="dsa_pre",
    )(h, g.reshape(1, d), w_in, g_q.reshape(1, -1), g_kv.reshape(1, -1), w_uq, w_uk, w_qidx)


def _t5_bucket_np(dist):
    max_exact = REL_BUCKETS // 2
    n = np.maximum(dist, 0)
    nf = np.maximum(n, 1).astype(np.float32)
    large = max_exact + (np.log(nf / np.float32(max_exact)) / np.float32(math.log(REL_MAX_DIST / max_exact))
                         * np.float32(REL_BUCKETS - max_exact)).astype(np.int32)
    large = np.minimum(large, REL_BUCKETS - 1)
    return np.where(n < max_exact, n, large).astype(np.int32)


def _bias_tile_buckets():
    i = np.arange(DSA_TQ)[:, None]
    j = np.arange(DSA_TQ)[None, :]
    d0 = _t5_bucket_np(i - j)
    d1 = _t5_bucket_np(DSA_TQ + i - j)
    far = _t5_bucket_np(np.full((DSA_TQ, DSA_TQ), 2 * DSA_TQ + 1))
    assert (far == _t5_bucket_np(np.full((DSA_TQ, DSA_TQ), 10 ** 6))).all()
    kinds = [np.concatenate(p, axis=1) for p in ((far, far), (d0, far), (far, d1), (d1, d0))]
    return np.stack(kinds).astype(np.int32)


def _far_bucket():
    return int(_t5_bucket_np(np.array(2 * DSA_TQ + 1)))


def _bias_expand_kernel(idx_ref, rb_ref, o_ref):
    hd = pl.program_id(1)
    idx = idx_ref[0]
    out = jnp.zeros(idx.shape, F32)
    for b in range(REL_BUCKETS):
        out = jnp.where(idx == b, rb_ref[b, hd], out)
    o_ref[0, 0] = (out - rb_ref[_far_bucket(), hd]) * LOG2E


def _bias_tiles(rel_bias):
    idx = jnp.asarray(_bias_tile_buckets())
    return pl.pallas_call(
        _bias_expand_kernel,
        grid=(4, ATT_HEADS),
        in_specs=[
            pl.BlockSpec((1, DSA_TQ, DSA_TK), lambda k, hd: (k, 0, 0)),
            pl.BlockSpec(memory_space=pltpu.SMEM),
        ],
        out_specs=pl.BlockSpec((1, 1, DSA_TQ, DSA_TK), lambda k, hd: (k, hd, 0, 0)),
        out_shape=jax.ShapeDtypeStruct((4, ATT_HEADS, DSA_TQ, DSA_TK), F32),
        name="dsa_bias_tiles",
    )(idx, rel_bias)


def _sortable(x):
    bits = lax.bitcast_convert_type(x, jnp.int32)
    return jnp.where(bits < 0, bits ^ jnp.int32(0x7FFFFFFF), bits)


def _dsa_attn_kernel(qlat_ref, qidx_ref, tail_ref, ckv_ref, kidx_ref, bias_ref, o_ref,
                     keys_ref, keyst_ref, wb_ref, m_ref, l_ref, alpha_ref, acc_ref, lg_ref, p_ref,
                     *, topk):
    qb = pl.program_id(1)
    tq, tk = DSA_TQ, DSA_TK
    rows = ATT_HEADS * tq
    n_kt = qb // (tk // tq) + 1
    q_pos = qb * tq + lax.broadcasted_iota(jnp.int32, (tq, tk), 0)
    k_off = lax.broadcasted_iota(jnp.int32, (tq, tk), 1)

    tail = tail_ref[...]
    w_scale = (IDX_HEADS ** -0.5) * (IDX_DIM ** -0.5)
    for hd in range(IDX_HEADS):
        col = tail[:, IDX_DIM + hd:IDX_DIM + hd + 1] * w_scale
        wb_ref[hd * tq:(hd + 1) * tq, :] = jnp.broadcast_to(col, (tq, LANES))

    def score_pair(pair, carry):
        for t in range(2):
            kt = 2 * pair + t
            start = pl.multiple_of(kt * tk, tk)
            kk = kidx_ref[pl.ds(start, tk), :]
            logits = jnp.maximum(_dot_nt(qidx_ref[0], kk), 0.0)
            wb = wb_ref[...]
            weighted = jnp.concatenate(
                [logits[:, c * LANES:(c + 1) * LANES] * wb for c in range(tk // LANES)], axis=1)
            score = jnp.sum(weighted.reshape(IDX_HEADS, tq, tk), axis=0)
            score = jnp.where(start + k_off <= q_pos, score, -jnp.inf)
            keys_ref[kt] = _sortable(score)
            keyst_ref[kt] = _sortable(score.T)
        return carry

    lax.fori_loop(0, (n_kt + 1) // 2, score_pair, 0)

    reps = tk // LANES

    def count_ge(cand):
        def body(pair, cnt):
            ge = (keyst_ref[pl.ds(2 * pair, 2)] >= cand).astype(jnp.int32)
            return cnt + jnp.sum(ge.reshape(2 * tk // SUBLANES, SUBLANES, tq), axis=0)
        cnt = lax.fori_loop(0, (n_kt + 1) // 2, body, jnp.zeros((SUBLANES, tq), jnp.int32))
        return jnp.sum(cnt, axis=0, keepdims=True)

    zero = jnp.zeros((1, tq), jnp.int32)
    thr0 = jnp.where(count_ge(zero) >= topk, zero, jnp.full((1, tq), INT_MIN, jnp.int32))

    def bit_step(it, t):
        cand = t | jnp.left_shift(jnp.int32(1), 30 - it)
        return jnp.where(count_ge(cand) >= topk, cand, t)

    thr_row = lax.fori_loop(0, 31, bit_step, thr0)
    thr = lax.bitcast_convert_type(
        jnp.broadcast_to(lax.bitcast_convert_type(thr_row, F32), (LANES, tq)).T, jnp.int32)

    m_ref[...] = jnp.full(m_ref.shape, MASKED, F32)
    l_ref[...] = jnp.zeros_like(l_ref)
    acc_ref[...] = jnp.zeros_like(acc_ref)
    parity = qb % 2

    def head_update(hd):
        r = slice(hd * tq, (hd + 1) * tq)
        lg = lg_ref[r, :]
        m_prev = m_ref[r, :]
        tile_max = jnp.maximum(lg[:, :LANES], lg[:, LANES:2 * LANES])
        for c in range(2, reps):
            tile_max = jnp.maximum(tile_max, lg[:, c * LANES:(c + 1) * LANES])
        m_new = jnp.maximum(m_prev, jnp.max(tile_max, axis=-1, keepdims=True))
        alpha = jnp.exp2(m_prev - m_new)
        p = jnp.exp2(lg - jnp.concatenate([m_new] * reps, axis=1))
        tile_sum = p[:, :LANES] + p[:, LANES:2 * LANES]
        for c in range(2, reps):
            tile_sum = tile_sum + p[:, c * LANES:(c + 1) * LANES]
        l_ref[r, :] = alpha * l_ref[r, :] + jnp.sum(tile_sum, axis=-1, keepdims=True)
        m_ref[r, :] = m_new
        alpha_ref[r, :] = alpha
        p_ref[r, :] = p.astype(BF16)

    def attend(kt, near):
        start = pl.multiple_of(kt * tk, tk)
        kv = ckv_ref[pl.ds(start, tk), :]
        ks = keys_ref[kt]
        sel = jnp.concatenate(
            [ks[:, c * LANES:(c + 1) * LANES] >= thr for c in range(reps)], axis=1)
        if near:
            sel = jnp.logical_and(sel, start + k_off <= q_pos)
        shift = jnp.where(sel, 0.0, MASKED)[None]
        if near:
            kind = jnp.where(kt == n_kt - 1, 1 + 2 * parity, 2 * (1 - parity))
            shift = shift + bias_ref[kind]
        logits = _dot_nt(qlat_ref[0], kv).reshape(ATT_HEADS, tq, tk) + shift
        lg_ref[...] = logits.reshape(rows, tk)
        for hd in range(ATT_HEADS):
            head_update(hd)
        alpha = alpha_ref[...]
        acc_ref[...] = (acc_ref[...] * jnp.concatenate([alpha] * (KV_LORA // LANES), axis=1)
                        + _dot(p_ref[...], kv))

    n_far = jnp.maximum(n_kt - 2, 0)

    def far_tile(kt, carry):
        attend(kt, False)
        return carry

    def near_tile(kt, carry):
        attend(kt, True)
        return carry

    lax.fori_loop(0, n_far, far_tile, 0)
    lax.fori_loop(n_far, n_kt, near_tile, 0)

    inv = 1.0 / l_ref[...]
    out = acc_ref[...] * jnp.concatenate([inv] * (KV_LORA // LANES), axis=1)
    for hd in range(ATT_HEADS):
        o_ref[:, hd * KV_LORA:(hd + 1) * KV_LORA] = out[hd * tq:(hd + 1) * tq, :].astype(BF16)


def _dsa_attn(qlat, qidx, tail, ckv, kidx, bias, *, batch, seq):
    tq, tk = DSA_TQ, DSA_TK
    assert seq % tk == 0 and tq == LANES and tk == 2 * tq
    nqb = seq // tq
    rows = ATT_HEADS * tq
    topk = min(TOPK_MAX, seq // 4)
    tokens = batch * seq
    qlat = qlat.reshape(batch * nqb, rows, KV_LORA)
    qidx = qidx.reshape(batch * nqb, rows, LANES)
    kern = functools.partial(_dsa_attn_kernel, topk=topk)
    return pl.pallas_call(
        kern,
        grid=(batch, nqb),
        in_specs=[
            pl.BlockSpec((1, rows, KV_LORA), lambda b, q: (b * nqb + q, 0, 0)),
            pl.BlockSpec((1, rows, LANES), lambda b, q: (b * nqb + q, 0, 0)),
            pl.BlockSpec((tq, LANES), lambda b, q: (b * nqb + q, 0)),
            pl.BlockSpec((seq, KV_LORA), lambda b, q: (b, 0)),
            pl.BlockSpec((seq, LANES), lambda b, q: (b, 0)),
            _const_spec((4, ATT_HEADS, tq, tk)),
        ],
        out_specs=pl.BlockSpec((tq, ATT_HEADS * KV_LORA), lambda b, q: (b * nqb + q, 0)),
        out_shape=jax.ShapeDtypeStruct((tokens, ATT_HEADS * KV_LORA), BF16),
        scratch_shapes=[
            pltpu.VMEM((seq // tk, tq, tk), jnp.int32),
            pltpu.VMEM((seq // tk, tk, tq), jnp.int32),
            pltpu.VMEM((rows, LANES), F32),
            pltpu.VMEM((rows, LANES), F32),
            pltpu.VMEM((rows, LANES), F32),
            pltpu.VMEM((rows, LANES), F32),
            pltpu.VMEM((rows, KV_LORA), F32),
            pltpu.VMEM((rows, tk), F32),
            pltpu.VMEM((rows, tk), BF16),
        ],
        compiler_params=_params("parallel", "arbitrary"),
        name="dsa_attn",
    )(qlat, qidx, tail, ckv, kidx, bias)


def _dsa_post_kernel(olat_ref, h_ref, wuv_ref, wo_ref, o_ref, heads_ref):
    for hd in range(ATT_HEADS):
        oh = _dot(olat_ref[:, hd * KV_LORA:(hd + 1) * KV_LORA], wuv_ref[hd])
        heads_ref[:, hd * ATT_HEAD_DIM:(hd + 1) * ATT_HEAD_DIM] = oh.astype(BF16)
    o_ref[...] = h_ref[...] + _dot(heads_ref[...], wo_ref[...])


def _dsa_post(olat, h, w_uv, w_o, tm=256):
    tokens, d = h.shape
    hq = ATT_HEADS * ATT_HEAD_DIM
    return pl.pallas_call(
        _dsa_post_kernel,
        grid=(tokens // tm,),
        in_specs=[
            pl.BlockSpec((tm, ATT_HEADS * KV_LORA), lambda i: (i, 0)),
            pl.BlockSpec((tm, d), lambda i: (i, 0)),
            _const_spec((ATT_HEADS, KV_LORA, ATT_HEAD_DIM)),
            _const_spec((hq, d)),
        ],
        out_specs=pl.BlockSpec((tm, d), lambda i: (i, 0)),
        out_shape=jax.ShapeDtypeStruct((tokens, d), F32),
        scratch_shapes=[pltpu.VMEM((tm, hq), BF16)],
        compiler_params=_params("parallel"),
        name="dsa_post",
    )(olat, h, w_uv, w_o)


def _dsa_layer(h, g, rel_bias, w_in, g_q, g_kv, w_uq, w_qidx, w_uk, w_uv, w_o, *, batch, seq):
    d = h.shape[1]
    w_in_p = jnp.pad(w_in, ((0, 0), (0, A_IN_PAD - w_in.shape[1]))).astype(BF16)
    w_qidx_p = jnp.pad(w_qidx.reshape(Q_LORA, IDX_HEADS, IDX_DIM),
                       ((0, 0), (0, 0), (0, LANES - IDX_DIM))).reshape(Q_LORA, IDX_HEADS * LANES)
    qlat, qidx, ckv, kidx, tail = _dsa_pre(
        h, g, w_in_p, g_q, g_kv, w_uq.astype(BF16), w_uk.astype(BF16), w_qidx_p.astype(BF16))
    bias = _bias_tiles(rel_bias)
    olat = _dsa_attn(qlat, qidx, tail, ckv, kidx, bias, batch=batch, seq=seq)
    return _dsa_post(olat, h, w_uv.astype(BF16), w_o.astype(BF16))


def _hg_proj_kernel(h_ref, g_ref, w_ref, lb_ref, o_ref, xn_ref, *, blocks_per_part):
    j = pl.program_id(1)

    @pl.when(j == 0)
    def _():
        xn_ref[...] = _rms(h_ref[...], g_ref[...]).astype(BF16)

    part = j // blocks_per_part

    @pl.when(jnp.logical_or(part == 0, part == 3))
    def _():
        o_ref[...] = _silu(_dot(xn_ref[...], w_ref[...]))

    @pl.when(part == 1)
    def _():
        lb = lb_ref[...]
        o_ref[...] = lb + (1.0 - lb) * _sigmoid(_dot(xn_ref[...], w_ref[...]))

    @pl.when(part == 2)
    def _():
        o_ref[...] = _dot(xn_ref[...], w_ref[...])


def _hg_proj(h, g, w_in, lb, tm=512, tn=2048):
    tokens, d = h.shape
    n = w_in.shape[1]
    bpp = d // tn
    kern = functools.partial(_hg_proj_kernel, blocks_per_part=bpp)
    return pl.pallas_call(
        kern,
        grid=(tokens // tm, n // tn),
        in_specs=[
            pl.BlockSpec((tm, d), lambda i, j: (i, 0)),
            pl.BlockSpec((1, d), lambda i, j: (0, 0)),
            pl.BlockSpec((d, tn), lambda i, j: (0, j)),
            pl.BlockSpec((1, tn), lambda i, j: (0, j % bpp)),
        ],
        out_specs=pl.BlockSpec((tm, tn), lambda i, j: (i, j)),
        out_shape=jax.ShapeDtypeStruct((tokens, n), F32),
        scratch_shapes=[pltpu.VMEM((tm, d), BF16)],
        compiler_params=_params("parallel", "arbitrary"),
        name="hgrn2_proj",
    )(h, g.reshape(1, d), w_in, lb.reshape(1, d))


def _hg_rec_kernel(q_ref, f_ref, v_ref, tri_ref, o_ref, a_ref, state_ref, as_ref, ks_ref, vs_ref):
    tt = pl.program_id(1)
    rows, width = q_ref.shape
    dk = width // HG_HEADS
    sub = HG_SUB

    @pl.when(tt == 0)
    def _():
        state_ref[...] = jnp.zeros_like(state_ref)

    log_f = jnp.log(f_ref[...])
    tri = tri_ref[...]
    a_sum = None
    for _ in range(3):
        part = log_f.astype(BF16)
        log_f = log_f - part.astype(F32)
        term = _dot(tri, part)
        a_sum = term if a_sum is None else a_sum + term
    a_ref[...] = a_sum

    def step(j, carry):
        r0 = pl.multiple_of(j * sub, sub)
        groups = sub // SUBLANES
        as_ref[...] = a_ref[pl.ds(r0, sub), :]
        ks_ref[...] = 1.0 - f_ref[pl.ds(r0, sub), :]
        vs_ref[...] = v_ref[pl.ds(r0, sub), :]
        for hd in range(HG_HEADS):
            sl = slice(hd * dk, (hd + 1) * dk)
            a = as_ref[:, sl]
            q = q_ref[pl.ds(r0, sub), sl]
            k = ks_ref[:, sl]
            v = vs_ref[:, sl]
            a_last = a[sub - 1:sub, :]

            intra = [jnp.zeros((SUBLANES, dk), F32) for _ in range(groups)]
            for s in range(sub):
                g0 = s // SUBLANES
                lo = g0 * SUBLANES
                t_idx = lo + lax.broadcasted_iota(jnp.int32, (sub - lo, 1), 0)
                a_s = as_ref[s:s + 1, sl]
                k_s = ks_ref[s:s + 1, sl]
                v_s = vs_ref[s:s + 1, sl]
                e = jnp.exp(jnp.where(t_idx >= s, a[lo:, :] - a_s, -jnp.inf))
                w = q[lo:, :] * (k_s * e)
                upd = jnp.sum(w, axis=-1, keepdims=True) * v_s
                for g in range(g0, groups):
                    rows_g = slice((g - g0) * SUBLANES, (g - g0 + 1) * SUBLANES)
                    intra[g] = intra[g] + upd[rows_g, :]

            st = state_ref[hd]
            q_dec = (q * jnp.exp(a)).astype(BF16)
            k_dec = (k * jnp.exp(a_last - a)).astype(BF16)
            inter = _dot_nt(q_dec, st.astype(BF16))
            o_ref[pl.ds(r0, sub), sl] = inter + jnp.concatenate(intra, axis=0)
            state_ref[hd] = st * jnp.exp(a_last) + _dot_tn(v.astype(BF16), k_dec)
        return carry

    lax.fori_loop(0, rows // sub, step, 0)


def _hg_rec(proj, *, batch, seq, d, tt=256):
    tokens = batch * seq
    tt = min(tt, seq)
    nt = seq // tt
    dk = d // HG_HEADS
    tri = (np.arange(tt)[:, None] >= np.arange(tt)[None, :]) & (
        np.arange(tt)[:, None] // HG_SUB == np.arange(tt)[None, :] // HG_SUB)
    tri = jnp.asarray(tri.astype(np.float32)).astype(BF16)
    return pl.pallas_call(
        _hg_rec_kernel,
        grid=(batch, nt),
        in_specs=[
            pl.BlockSpec((tt, d), lambda b, t: (b * nt + t, 0)),
            pl.BlockSpec((tt, d), lambda b, t: (b * nt + t, 1)),
            pl.BlockSpec((tt, d), lambda b, t: (b * nt + t, 2)),
            _const_spec((tt, tt)),
        ],
        out_specs=pl.BlockSpec((tt, d), lambda b, t: (b * nt + t, 0)),
        out_shape=jax.ShapeDtypeStruct((tokens, d), F32),
        scratch_shapes=[pltpu.VMEM((tt, d), F32), pltpu.VMEM((HG_HEADS, dk, dk), F32)]
        + [pltpu.VMEM((HG_SUB, d), F32)] * 3,
        compiler_params=_params("parallel", "arbitrary"),
        name="hgrn2_recurrence",
    )(proj, proj, proj, tri)


def _hg_post_kernel(o_ref, gate_ref, h_ref, gn_ref, w_ref, out_ref, y_ref):
    width = o_ref.shape[1]
    dv = width // HG_HEADS
    for hd in range(HG_HEADS):
        sl = slice(hd * dv, (hd + 1) * dv)
        o = o_ref[:, sl]
        o = o * lax.rsqrt(jnp.mean(o * o, axis=-1, keepdims=True) + EPS)
        y_ref[:, sl] = (o * gn_ref[:, sl] * gate_ref[:, sl]).astype(BF16)
    out_ref[...] = h_ref[...] + _dot(y_ref[...], w_ref[...])


def _hg_post(o, proj, h, g_norm, w_o, tm=256):
    tokens, d = h.shape
    return pl.pallas_call(
        _hg_post_kernel,
        grid=(tokens // tm,),
        in_specs=[
            pl.BlockSpec((tm, d), lambda i: (i, 0)),
            pl.BlockSpec((tm, d), lambda i: (i, 3)),
            pl.BlockSpec((tm, d), lambda i: (i, 0)),
            _const_spec((1, d)),
            _const_spec((d, d)),
        ],
        out_specs=pl.BlockSpec((tm, d), lambda i: (i, 0)),
        out_shape=jax.ShapeDtypeStruct((tokens, d), F32),
        scratch_shapes=[pltpu.VMEM((tm, d), BF16)],
        compiler_params=_params("parallel"),
        name="hgrn2_post",
    )(o, proj, h, g_norm.reshape(1, d), w_o)


def _hgrn2_layer(h, g, w_in, lb, g_norm, w_o, *, batch, seq):
    d = h.shape[1]
    proj = _hg_proj(h, g, w_in.astype(BF16), lb)
    o = _hg_rec(proj, batch=batch, seq=seq, d=d)
    return _hg_post(o, proj, h, g_norm, w_o.astype(BF16))


def _lower_bounds_kernel(b_ref, o_ref):
    b = b_ref[...]
    e = jnp.exp(b - jnp.max(b, axis=0, keepdims=True))
    soft = e / jnp.sum(e, axis=0, keepdims=True)
    run = jnp.zeros_like(soft[0:1])
    for layer in range(soft.shape[0]):
        if layer > 0:
            run = run + soft[layer:layer + 1]
        o_ref[layer:layer + 1, :] = run


def _lower_bounds(b):
    return pl.pallas_call(
        _lower_bounds_kernel,
        out_shape=jax.ShapeDtypeStruct(b.shape, F32),
        name="hgrn2_lower_bounds",
    )(b)


POOL_HALO = 16


def _pool_kernel(h_ref, halo_ref, g_ref, w_ref, sc_ref, o_ref, *, tiles_per_seq):
    i = pl.program_id(0)
    tm, d = h_ref.shape
    group = d // len(POOL_WINDOWS)
    g = g_ref[...]
    u = _rms(h_ref[...], g)
    hu = _rms(halo_ref[...], g) * _not_first(i, tiles_per_seq)
    ext = jnp.concatenate([hu, u], axis=0)
    pos = (i % tiles_per_seq) * tm + lax.broadcasted_iota(jnp.int32, (tm, group), 0)
    for gi, win in enumerate(POOL_WINDOWS):
        sl = slice(gi * group, (gi + 1) * group)
        s = ext[:, sl]
        span = 1
        while span < win:
            s = s + pltpu.roll(s, span, axis=0)
            span *= 2
        count = jnp.minimum(pos + 1, win).astype(F32)
        diff = s[POOL_HALO:, :] / count - u[:, sl]
        y = _dot(diff.astype(BF16), w_ref[gi])
        o_ref[:, sl] = h_ref[:, sl] + y * sc_ref[:, sl]


def _pool_layer(h, g, w_group, scale, *, seq, tm=256):
    tokens, d = h.shape
    tm = min(tm, seq)
    assert max(POOL_WINDOWS) <= POOL_HALO
    kern = functools.partial(_pool_kernel, tiles_per_seq=seq // tm)
    return pl.pallas_call(
        kern,
        grid=(tokens // tm,),
        in_specs=[
            pl.BlockSpec((tm, d), lambda i: (i, 0)),
            pl.BlockSpec((POOL_HALO, d), lambda i: (jnp.maximum(i * (tm // POOL_HALO) - 1, 0), 0)),
            _const_spec((1, d)),
            _const_spec(w_group.shape),
            _const_spec((1, d)),
        ],
        out_specs=pl.BlockSpec((tm, d), lambda i: (i, 0)),
        out_shape=jax.ShapeDtypeStruct((tokens, d), F32),
        compiler_params=_params("parallel"),
        name="pool_mixer",
    )(h, h, g.reshape(1, d), w_group.astype(BF16), scale.reshape(1, d))


CONF_HALO = 32


def _conf_glu_kernel(h_ref, g_ref, wa_ref, wg_ref, ba_ref, bg_ref, o_ref, xn_ref):
    @pl.when(pl.program_id(1) == 0)
    def _():
        xn_ref[...] = _rms(h_ref[...], g_ref[...]).astype(BF16)

    xn = xn_ref[...]
    a = _dot(xn, wa_ref[...]) + ba_ref[...]
    gate = _dot(xn, wg_ref[...]) + bg_ref[...]
    o_ref[...] = a * _sigmoid(gate)


def _conf_glu(h, g, w_pw1, b_pw1, tm=512, tn=1024):
    tokens, d = h.shape
    nj = d // tn
    b = b_pw1.reshape(1, -1)
    return pl.pallas_call(
        _conf_glu_kernel,
        grid=(tokens // tm, nj),
        in_specs=[
            pl.BlockSpec((tm, d), lambda i, j: (i, 0)),
            pl.BlockSpec((1, d), lambda i, j: (0, 0)),
            pl.BlockSpec((d, tn), lambda i, j: (0, j)),
            pl.BlockSpec((d, tn), lambda i, j: (0, nj + j)),
            pl.BlockSpec((1, tn), lambda i, j: (0, j)),
            pl.BlockSpec((1, tn), lambda i, j: (0, nj + j)),
        ],
        out_specs=pl.BlockSpec((tm, tn), lambda i, j: (i, j)),
        out_shape=jax.ShapeDtypeStruct((tokens, d), F32),
        scratch_shapes=[pltpu.VMEM((tm, d), BF16)],
        compiler_params=_params("parallel", "arbitrary"),
        name="conformer_glu",
    )(h, g.reshape(1, d), w_pw1, w_pw1, b, b)


def _conf_conv_kernel(u_ref, halo_ref, h_ref, wdw_ref, bdw_ref, lng_ref, lnb_ref, w2_ref, b2_ref,
                      o_ref, ext_ref, c_ref, *, tiles_per_seq):
    i = pl.program_id(0)
    tm, d = u_ref.shape
    width = wdw_ref.shape[0]
    ext_ref[:CONF_HALO, :] = halo_ref[...] * _not_first(i, tiles_per_seq)
    ext_ref[CONF_HALO:, :] = u_ref[...]
    base = CONF_HALO - (width - 1)
    n_ext = tm + CONF_HALO
    for c in range(d // LANES):
        sl = slice(c * LANES, (c + 1) * LANES)
        ext = ext_ref[:, sl]
        acc = jnp.broadcast_to(bdw_ref[:, sl], (tm, LANES))
        for phase in range(SUBLANES):
            taps = [j for j in range(width) if (base + j) % SUBLANES == phase]
            if not taps:
                continue
            shifted = ext if phase == 0 else pltpu.roll(ext, n_ext - phase, axis=0)
            for j in taps:
                off = base + j - phase
                acc = acc + wdw_ref[j:j + 1, sl] * shifted[off:off + tm, :]
        c_ref[:, sl] = acc
    x = c_ref[...]
    mu = jnp.mean(x, axis=-1, keepdims=True)
    xc = x - mu
    y = xc * lax.rsqrt(jnp.mean(xc * xc, axis=-1, keepdims=True) + EPS)
    y = _silu(y * lng_ref[...] + lnb_ref[...]).astype(BF16)
    o_ref[...] = h_ref[...] + _dot(y, w2_ref[...]) + b2_ref[...]


def _conf_conv(u, h, w_dw, b_dw, ln_g, ln_b, w_pw2, b_pw2, *, seq, tm=256):
    tokens, d = h.shape
    tm = min(tm, seq)
    width = w_dw.shape[0]
    assert width - 1 <= CONF_HALO
    kern = functools.partial(_conf_conv_kernel, tiles_per_seq=seq // tm)
    return pl.pallas_call(
        kern,
        grid=(tokens // tm,),
        in_specs=[
            pl.BlockSpec((tm, d), lambda i: (i, 0)),
            pl.BlockSpec((CONF_HALO, d), lambda i: (jnp.maximum(i * (tm // CONF_HALO) - 1, 0), 0)),
            pl.BlockSpec((tm, d), lambda i: (i, 0)),
            _const_spec((width, d)),
            _const_spec((1, d)),
            _const_spec((1, d)),
            _const_spec((1, d)),
            _const_spec((d, d)),
            _const_spec((1, d)),
        ],
        out_specs=pl.BlockSpec((tm, d), lambda i: (i, 0)),
        out_shape=jax.ShapeDtypeStruct((tokens, d), F32),
        scratch_shapes=[pltpu.VMEM((tm + CONF_HALO, d), F32), pltpu.VMEM((tm, d), F32)],
        compiler_params=_params("parallel"),
        name="conformer_conv",
    )(u, u, h, w_dw, b_dw.reshape(1, d), ln_g.reshape(1, d), ln_b.reshape(1, d), w_pw2,
      b_pw2.reshape(1, d))


def _conformer_layer(h, g, w_pw1, b_pw1, w_dw, b_dw, ln_g, ln_b, w_pw2, b_pw2, *, seq):
    u = _conf_glu(h, g, w_pw1.astype(BF16), b_pw1)
    return _conf_conv(u, h, w_dw, b_dw, ln_g, ln_b, w_pw2.astype(BF16), b_pw2, seq=seq)


def kernel(x, rel_bias, a_w_in, a_g_q, a_g_kv, a_w_uq, a_w_qidx, a_w_uk, a_w_uv, a_w_o, b_w_in, b_lower_bounds, b_g_norm, b_w_o, c_w_group, c_scale, d_w_pw1, d_b_pw1, d_w_dw, d_b_dw, d_ln_g, d_ln_b, d_w_pw2, d_b_pw2, norm_mix, norm_ffn, ffn_w_up, ffn_w_conv, ffn_b_conv, ffn_w_down, final_norm):
    batch, seq, d = x.shape
    depth = norm_mix.shape[0]
    lb_all = _lower_bounds(b_lower_bounds)
    h = x.reshape(batch * seq, d)
    for i in range(depth):
        j, kind = divmod(i, 4)
        g = norm_mix[i]
        if kind == 0:
            h = _dsa_layer(h, g, rel_bias, a_w_in[j], a_g_q[j], a_g_kv[j], a_w_uq[j], a_w_qidx[j],
                           a_w_uk[j], a_w_uv[j], a_w_o[j], batch=batch, seq=seq)
        elif kind == 1:
            h = _hgrn2_layer(h, g, b_w_in[j], lb_all[i], b_g_norm[j], b_w_o[j], batch=batch, seq=seq)
        elif kind == 2:
            h = _pool_layer(h, g, c_w_group[j], c_scale[j], seq=seq)
        else:
            h = _conformer_layer(h, g, d_w_pw1[j], d_b_pw1[j], d_w_dw[j], d_b_dw[j], d_ln_g[j],
                                 d_ln_b[j], d_w_pw2[j], d_b_pw2[j], seq=seq)
        h = _ffn_layer(h, norm_ffn[i], ffn_w_up[i].astype(BF16), ffn_w_conv[i], ffn_b_conv[i],
                       ffn_w_down[i].astype(BF16), final_norm, seq=seq,
                       final_norm=(i == depth - 1))
    return h.reshape(batch, seq, d)
```

```python
import functools
import math

import jax
import jax.numpy as jnp
import numpy as np
from jax import lax
from jax.experimental import pallas as pl
from jax.experimental.pallas import tpu as pltpu

F32 = jnp.float32
BF16 = jnp.bfloat16

EPS = 1e-6
ATT_HEADS = 16
ATT_HEAD_DIM = 128
Q_LORA = 512
KV_LORA = 256
IDX_HEADS = 16
IDX_DIM = 64
TOPK_MAX = 256
REL_BUCKETS = 32
REL_MAX_DIST = 128
HG_HEADS = 16
POOL_WINDOWS = (2, 4, 8, 16)

LANES = 128
SUBLANES = 8
VMEM_LIMIT = 56 * 1024 * 1024
MASKED = -1e30
INT_MIN = -2 ** 31

A_IN_PAD = 896
DSA_TQ = 128
DSA_TK = 256
DSA_PRE_BLOCKS = 4
HG_SUB = 16
LOG2E = math.log2(math.e)


def _dot(a, b):
    return jnp.dot(a, b, preferred_element_type=F32)


def _dot_nt(a, b):
    return lax.dot_general(a, b, (((1,), (1,)), ((), ())), preferred_element_type=F32)


def _dot_tn(a, b):
    return lax.dot_general(a, b, (((0,), (0,)), ((), ())), preferred_element_type=F32)


def _rms(x, g):
    return x * lax.rsqrt(jnp.mean(x * x, axis=-1, keepdims=True) + EPS) * g


def _sigmoid(x):
    return 1.0 / (1.0 + jnp.exp(-x))


def _silu(x):
    return x * _sigmoid(x)


def _not_first(tile, tiles_per_seq):
    return jnp.where(tile % tiles_per_seq != 0, 1.0, 0.0).astype(F32)


def _params(*sem):
    return pltpu.CompilerParams(dimension_semantics=sem, vmem_limit_bytes=VMEM_LIMIT)


def _const_spec(shape):
    nd = len(shape)
    return pl.BlockSpec(shape, lambda *_: (0,) * nd)


FFN_HALO = 16


def _ffn_kernel(h_ref, halo_ref, g_ref, wa_ref, wb_ref, ca_ref, cb_ref, ba_ref, bb_ref, wd_ref,
                gf_ref, o_ref, xn_ref, acc_ref, *, tiles_per_seq, final_norm):
    i = pl.program_id(0)
    f = pl.program_id(1)
    tm = h_ref.shape[0]

    @pl.when(f == 0)
    def _():
        g = g_ref[...]
        xn_ref[FFN_HALO:, :] = _rms(h_ref[...], g).astype(BF16)
        hn = _rms(halo_ref[...], g) * _not_first(i, tiles_per_seq)
        xn_ref[:FFN_HALO, :] = hn.astype(BF16)
        acc_ref[...] = jnp.zeros_like(acc_ref)

    xn = xn_ref[...]
    width = ca_ref.shape[0]

    def conv(u, w_ref, b_ref):
        w = w_ref[...]
        out = b_ref[...] + w[width - 1:width, :] * u[FFN_HALO:, :]
        for j in range(width - 1):
            shift = width - 1 - j
            out = out + w[j:j + 1, :] * pltpu.roll(u, shift, axis=0)[FFN_HALO:, :]
        return out

    a = conv(_dot(xn, wa_ref[...]), ca_ref, ba_ref)
    b = conv(_dot(xn, wb_ref[...]), cb_ref, bb_ref)
    gated = (_silu(a) * b).astype(BF16)
    acc_ref[...] += _dot(gated, wd_ref[...])

    @pl.when(f == pl.num_programs(1) - 1)
    def _():
        out = h_ref[...] + acc_ref[...]
        if final_norm:
            out = _rms(out, gf_ref[...])
        o_ref[...] = out


def _ffn_layer(h, g, w_up, w_conv, b_conv, w_down, gf, *, seq, final_norm, tm=512, tf=512):
    tokens, d = h.shape
    d_ff = w_down.shape[0]
    tm = min(tm, seq)
    nf = d_ff // tf
    width = w_conv.shape[0]
    assert d_ff % tf == 0 and seq % tm == 0 and width - 1 <= FFN_HALO
    kern = functools.partial(_ffn_kernel, tiles_per_seq=seq // tm, final_norm=final_norm)
    return pl.pallas_call(
        kern,
        grid=(tokens // tm, nf),
        in_specs=[
            pl.BlockSpec((tm, d), lambda i, f: (i, 0)),
            pl.BlockSpec((FFN_HALO, d), lambda i, f: (jnp.maximum(i * (tm // FFN_HALO) - 1, 0), 0)),
            pl.BlockSpec((1, d), lambda i, f: (0, 0)),
            pl.BlockSpec((d, tf), lambda i, f: (0, f)),
            pl.BlockSpec((d, tf), lambda i, f: (0, nf + f)),
            pl.BlockSpec((width, tf), lambda i, f: (0, f)),
            pl.BlockSpec((width, tf), lambda i, f: (0, nf + f)),
            pl.BlockSpec((1, tf), lambda i, f: (0, f)),
            pl.BlockSpec((1, tf), lambda i, f: (0, nf + f)),
            pl.BlockSpec((tf, d), lambda i, f: (f, 0)),
            pl.BlockSpec((1, d), lambda i, f: (0, 0)),
        ],
        out_specs=pl.BlockSpec((tm, d), lambda i, f: (i, 0)),
        out_shape=jax.ShapeDtypeStruct((tokens, d), F32),
        scratch_shapes=[pltpu.VMEM((tm + FFN_HALO, d), BF16), pltpu.VMEM((tm, d), F32)],
        compiler_params=_params("parallel", "arbitrary"),
        name="conv_ffn",
    )(h, h, g.reshape(1, d), w_up, w_up, w_conv, w_conv, b_conv.reshape(1, -1),
      b_conv.reshape(1, -1), w_down, gf.reshape(1, d))


def _dsa_pre_kernel(h_ref, g_ref, win_ref, gq_ref, gkv_ref, wuq_ref, wuk_ref, wqi_ref,
                    qlat_ref, qidx_ref, ckv_ref, kidx_ref, tail_ref):
    xn = _rms(h_ref[...], g_ref[...]).astype(BF16)
    p = _dot(xn, win_ref[...])
    cq = _rms(p[:, :Q_LORA], gq_ref[...]).astype(BF16)
    ckv_ref[...] = _rms(p[:, Q_LORA:Q_LORA + KV_LORA], gkv_ref[...]).astype(BF16)
    tail = p[:, Q_LORA + KV_LORA:]
    lane = lax.broadcasted_iota(jnp.int32, tail.shape, 1)
    kidx_ref[...] = jnp.where(lane < IDX_DIM, tail, 0.0).astype(BF16)
    tail_ref[...] = tail
    q = _dot(cq, wuq_ref[...]).astype(BF16)
    qi = _dot(cq, wqi_ref[...])
    q_scale = (ATT_HEAD_DIM ** -0.5) * LOG2E
    for hd in range(ATT_HEADS):
        sl = slice(hd * ATT_HEAD_DIM, (hd + 1) * ATT_HEAD_DIM)
        ql = (_dot(q[:, sl], wuk_ref[hd]) * q_scale).astype(BF16)
        qx = qi[:, sl].astype(BF16)
        for blk in range(DSA_PRE_BLOCKS):
            rows = slice(blk * DSA_TQ, (blk + 1) * DSA_TQ)
            qlat_ref[blk, hd] = ql[rows]
            qidx_ref[blk, hd] = qx[rows]


def _dsa_pre(h, g, w_in, g_q, g_kv, w_uq, w_uk, w_qidx):
    tokens, d = h.shape
    nq = DSA_PRE_BLOCKS
    tm = nq * DSA_TQ
    nblk = tokens // DSA_TQ
    hq = ATT_HEADS * ATT_HEAD_DIM
    return pl.pallas_call(
        _dsa_pre_kernel,
        grid=(tokens // tm,),
        in_specs=[
            pl.BlockSpec((tm, d), lambda i: (i, 0)),
            _const_spec((1, d)),
            _const_spec((d, A_IN_PAD)),
            _const_spec((1, Q_LORA)),
            _const_spec((1, KV_LORA)),
            _const_spec((Q_LORA, hq)),
            _const_spec((ATT_HEADS, ATT_HEAD_DIM, KV_LORA)),
            _const_spec((Q_LORA, IDX_HEADS * LANES)),
        ],
        out_specs=[
            pl.BlockSpec((nq, ATT_HEADS, DSA_TQ, KV_LORA), lambda i: (i, 0, 0, 0)),
            pl.BlockSpec((nq, IDX_HEADS, DSA_TQ, LANES), lambda i: (i, 0, 0, 0)),
            pl.BlockSpec((tm, KV_LORA), lambda i: (i, 0)),
            pl.BlockSpec((tm, LANES), lambda i: (i, 0)),
            pl.BlockSpec((tm, LANES), lambda i: (i, 0)),
        ],
        out_shape=[
            jax.ShapeDtypeStruct((nblk, ATT_HEADS, DSA_TQ, KV_LORA), BF16),
            jax.ShapeDtypeStruct((nblk, IDX_HEADS, DSA_TQ, LANES), BF16),
            jax.ShapeDtypeStruct((tokens, KV_LORA), BF16),
            jax.ShapeDtypeStruct((tokens, LANES), BF16),
            jax.ShapeDtypeStruct((tokens, LANES), F32),
        ],
        compiler_params=_params("parallel"),
        name="dsa_pre",
    )(h, g.reshape(1, d), w_in, g_q.reshape(1, -1), g_kv.reshape(1, -1), w_uq, w_uk, w_qidx)


def _t5_bucket_np(dist):
    max_exact = REL_BUCKETS // 2
    n = np.maximum(dist, 0)
    nf = np.maximum(n, 1).astype(np.float32)
    large = max_exact + (np.log(nf / np.float32(max_exact)) / np.float32(math.log(REL_MAX_DIST / max_exact))
                         * np.float32(REL_BUCKETS - max_exact)).astype(np.int32)
    large = np.minimum(large, REL_BUCKETS - 1)
    return np.where(n < max_exact, n, large).astype(np.int32)


def _bias_tile_buckets():
    i = np.arange(DSA_TQ)[:, None]
    j = np.arange(DSA_TQ)[None, :]
    d0 = _t5_bucket_np(i - j)
    d1 = _t5_bucket_np(DSA_TQ + i - j)
    far = _t5_bucket_np(np.full((DSA_TQ, DSA_TQ), 2 * DSA_TQ + 1))
    assert (far == _t5_bucket_np(np.full((DSA_TQ, DSA_TQ), 10 ** 6))).all()
    kinds = [np.concatenate(p, axis=1) for p in ((far, far), (d0, far), (far, d1), (d1, d0))]
    return np.stack(kinds).astype(np.int32)


def _far_bucket():
    return int(_t5_bucket_np(np.array(2 * DSA_TQ + 1)))


def _bias_expand_kernel(idx_ref, rb_ref, o_ref):
    hd = pl.program_id(1)
    idx = idx_ref[0]
    out = jnp.zeros(idx.shape, F32)
    for b in range(REL_BUCKETS):
        out = jnp.where(idx == b, rb_ref[b, hd], out)
    o_ref[0, 0] = (out - rb_ref[_far_bucket(), hd]) * LOG2E


def _bias_tiles(rel_bias):
    idx = jnp.asarray(_bias_tile_buckets())
    return pl.pallas_call(
        _bias_expand_kernel,
        grid=(4, ATT_HEADS),
        in_specs=[
            pl.BlockSpec((1, DSA_TQ, DSA_TK), lambda k, hd: (k, 0, 0)),
            pl.BlockSpec(memory_space=pltpu.SMEM),
        ],
        out_specs=pl.BlockSpec((1, 1, DSA_TQ, DSA_TK), lambda k, hd: (k, hd, 0, 0)),
        out_shape=jax.ShapeDtypeStruct((4, ATT_HEADS, DSA_TQ, DSA_TK), F32),
        name="dsa_bias_tiles",
    )(idx, rel_bias)


def _sortable(x):
    bits = lax.bitcast_convert_type(x, jnp.int32)
    return jnp.where(bits < 0, bits ^ jnp.int32(0x7FFFFFFF), bits)


def _dsa_attn_kernel(qlat_ref, qidx_ref, tail_ref, ckv_ref, kidx_ref, bias_ref, o_ref,
                     keys_ref, keyst_ref, wb_ref, m_ref, l_ref, alpha_ref, acc_ref, lg_ref, p_ref,
                     *, topk):
    qb = pl.program_id(1)
    tq, tk = DSA_TQ, DSA_TK
    rows = ATT_HEADS * tq
    n_kt = qb // (tk // tq) + 1
    q_pos = qb * tq + lax.broadcasted_iota(jnp.int32, (tq, tk), 0)
    k_off = lax.broadcasted_iota(jnp.int32, (tq, tk), 1)

    tail = tail_ref[...]
    w_scale = (IDX_HEADS ** -0.5) * (IDX_DIM ** -0.5)
    for hd in range(IDX_HEADS):
        col = tail[:, IDX_DIM + hd:IDX_DIM + hd + 1] * w_scale
        wb_ref[hd * tq:(hd + 1) * tq, :] = jnp.broadcast_to(col, (tq, LANES))

    def score_pair(pair, carry):
        for t in range(2):
            kt = 2 * pair + t
            start = pl.multiple_of(kt * tk, tk)
            kk = kidx_ref[pl.ds(start, tk), :]
            logits = jnp.maximum(_dot_nt(qidx_ref[0], kk), 0.0)
            wb = wb_ref[...]
            weighted = jnp.concatenate(
                [logits[:, c * LANES:(c + 1) * LANES] * wb for c in range(tk // LANES)], axis=1)
            score = jnp.sum(weighted.reshape(IDX_HEADS, tq, tk), axis=0)
            score = jnp.where(start + k_off <= q_pos, score, -jnp.inf)
            keys_ref[kt] = _sortable(score)
            keyst_ref[kt] = _sortable(score.T)
        return carry

    lax.fori_loop(0, (n_kt + 1) // 2, score_pair, 0)

    reps = tk // LANES

    def count_ge(cand):
        def body(pair, cnt):
            ge = (keyst_ref[pl.ds(2 * pair, 2)] >= cand).astype(jnp.int32)
            return cnt + jnp.sum(ge.reshape(2 * tk // SUBLANES, SUBLANES, tq), axis=0)
        cnt = lax.fori_loop(0, (n_kt + 1) // 2, body, jnp.zeros((SUBLANES, tq), jnp.int32))
        return jnp.sum(cnt, axis=0, keepdims=True)

    zero = jnp.zeros((1, tq), jnp.int32)
    thr0 = jnp.where(count_ge(zero) >= topk, zero, jnp.full((1, tq), INT_MIN, jnp.int32))

    def bit_step(it, t):
        cand = t | jnp.left_shift(jnp.int32(1), 30 - it)
        return jnp.where(count_ge(cand) >= topk, cand, t)

    thr_row = lax.fori_loop(0, 31, bit_step, thr0)
    thr = lax.bitcast_convert_type(
        jnp.broadcast_to(lax.bitcast_convert_type(thr_row, F32), (LANES, tq)).T, jnp.int32)

    m_ref[...] = jnp.full(m_ref.shape, MASKED, F32)
    l_ref[...] = jnp.zeros_like(l_ref)
    acc_ref[...] = jnp.zeros_like(acc_ref)
    parity = qb % 2

    def head_update(hd, slot):
        r = slice(hd * tq, (hd + 1) * tq)
        lg = lg_ref[slot, r, :]
        m_prev = m_ref[r, :]
        tile_max = jnp.maximum(lg[:, :LANES], lg[:, LANES:2 * LANES])
        for c in range(2, reps):
            tile_max = jnp.maximum(tile_max, lg[:, c * LANES:(c + 1) * LANES])
        m_new = jnp.maximum(m_prev, jnp.max(tile_max.astype(F32), axis=-1, keepdims=True))
        alpha = jnp.exp2(m_prev - m_new)
        p = jnp.exp2(lg - jnp.concatenate([m_new.astype(BF16)] * reps, axis=1))
        tile_sum = p[:, :LANES] + p[:, LANES:2 * LANES]
        for c in range(2, reps):
            tile_sum = tile_sum + p[:, c * LANES:(c + 1) * LANES]
        row_sum = jnp.sum(tile_sum.astype(F32), axis=-1, keepdims=True)
        l_ref[r, :] = alpha * l_ref[r, :] + row_sum
        m_ref[r, :] = m_new
        alpha_ref[slot, r, :] = alpha
        p_ref[slot, r, :] = p

    def attend(kt, near, slot):
        start = pl.multiple_of(kt * tk, tk)
        kv = ckv_ref[pl.ds(start, tk), :]
        ks = keys_ref[kt]
        sel = jnp.concatenate(
            [ks[:, c * LANES:(c + 1) * LANES] >= thr for c in range(reps)], axis=1)
        if near:
            sel = jnp.logical_and(sel, start + k_off <= q_pos)
        shift = jnp.where(sel, 0.0, MASKED)[None]
        if near:
            kind = jnp.where(kt == n_kt - 1, 1 + 2 * parity, 2 * (1 - parity))
            shift = shift + bias_ref[kind]
        logits = _dot_nt(qlat_ref[0], kv).reshape(ATT_HEADS, tq, tk) + shift
        lg_ref[slot] = logits.reshape(rows, tk).astype(BF16)
        for hd in range(ATT_HEADS):
            head_update(hd, slot)
        alpha = alpha_ref[slot]
        acc_ref[...] = (acc_ref[...] * jnp.concatenate([alpha] * (KV_LORA // LANES), axis=1)
                        + _dot(p_ref[slot], kv))

    n_far = jnp.maximum(n_kt - 2, 0)
    n_pairs = n_far // 2

    def far_pair(pair, carry):
        attend(2 * pair, False, 0)
        attend(2 * pair + 1, False, 1)
        return carry

    def far_tile(kt, carry):
        attend(kt, False, 0)
        return carry

    def near_tile(kt, carry):
        attend(kt, True, 0)
        return carry

    lax.fori_loop(0, n_pairs, far_pair, 0)
    lax.fori_loop(2 * n_pairs, n_far, far_tile, 0)
    lax.fori_loop(n_far, n_kt, near_tile, 0)

    inv = 1.0 / l_ref[...]
    out = acc_ref[...] * jnp.concatenate([inv] * (KV_LORA // LANES), axis=1)
    for hd in range(ATT_HEADS):
        o_ref[:, hd * KV_LORA:(hd + 1) * KV_LORA] = out[hd * tq:(hd + 1) * tq, :].astype(BF16)


def _dsa_attn(qlat, qidx, tail, ckv, kidx, bias, *, batch, seq):
    tq, tk = DSA_TQ, DSA_TK
    assert seq % tk == 0 and tq == LANES and tk == 2 * tq
    nqb = seq // tq
    rows = ATT_HEADS * tq
    topk = min(TOPK_MAX, seq // 4)
    tokens = batch * seq
    qlat = qlat.reshape(batch * nqb, rows, KV_LORA)
    qidx = qidx.reshape(batch * nqb, rows, LANES)
    kern = functools.partial(_dsa_attn_kernel, topk=topk)
    return pl.pallas_call(
        kern,
        grid=(batch, nqb),
        in_specs=[
            pl.BlockSpec((1, rows, KV_LORA), lambda b, q: (b * nqb + q, 0, 0)),
            pl.BlockSpec((1, rows, LANES), lambda b, q: (b * nqb + q, 0, 0)),
            pl.BlockSpec((tq, LANES), lambda b, q: (b * nqb + q, 0)),
            pl.BlockSpec((seq, KV_LORA), lambda b, q: (b, 0)),
            pl.BlockSpec((seq, LANES), lambda b, q: (b, 0)),
            _const_spec((4, ATT_HEADS, tq, tk)),
        ],
        out_specs=pl.BlockSpec((tq, ATT_HEADS * KV_LORA), lambda b, q: (b * nqb + q, 0)),
        out_shape=jax.ShapeDtypeStruct((tokens, ATT_HEADS * KV_LORA), BF16),
        scratch_shapes=[
            pltpu.VMEM((seq // tk, tq, tk), jnp.int32),
            pltpu.VMEM((seq // tk, tk, tq), jnp.int32),
            pltpu.VMEM((rows, LANES), F32),
            pltpu.VMEM((rows, LANES), F32),
            pltpu.VMEM((rows, LANES), F32),
            pltpu.VMEM((2, rows, LANES), F32),
            pltpu.VMEM((rows, KV_LORA), F32),
            pltpu.VMEM((2, rows, tk), BF16),
            pltpu.VMEM((2, rows, tk), BF16),
        ],
        compiler_params=_params("parallel", "arbitrary"),
        name="dsa_attn",
    )(qlat, qidx, tail, ckv, kidx, bias)


def _dsa_post_kernel(olat_ref, h_ref, wuv_ref, wo_ref, o_ref, heads_ref):
    for hd in range(ATT_HEADS):
        oh = _dot(olat_ref[:, hd * KV_LORA:(hd + 1) * KV_LORA], wuv_ref[hd])
        heads_ref[:, hd * ATT_HEAD_DIM:(hd + 1) * ATT_HEAD_DIM] = oh.astype(BF16)
    o_ref[...] = h_ref[...] + _dot(heads_ref[...], wo_ref[...])


def _dsa_post(olat, h, w_uv, w_o, tm=512):
    tokens, d = h.shape
    hq = ATT_HEADS * ATT_HEAD_DIM
    return pl.pallas_call(
        _dsa_post_kernel,
        grid=(tokens // tm,),
        in_specs=[
            pl.BlockSpec((tm, ATT_HEADS * KV_LORA), lambda i: (i, 0)),
            pl.BlockSpec((tm, d), lambda i: (i, 0)),
            _const_spec((ATT_HEADS, KV_LORA, ATT_HEAD_DIM)),
            _const_spec((hq, d)),
        ],
        out_specs=pl.BlockSpec((tm, d), lambda i: (i, 0)),
        out_shape=jax.ShapeDtypeStruct((tokens, d), F32),
        scratch_shapes=[pltpu.VMEM((tm, hq), BF16)],
        compiler_params=_params("parallel"),
        name="dsa_post",
    )(olat, h, w_uv, w_o)


def _dsa_layer(h, g, rel_bias, w_in, g_q, g_kv, w_uq, w_qidx, w_uk, w_uv, w_o, *, batch, seq):
    d = h.shape[1]
    w_in_p = jnp.pad(w_in, ((0, 0), (0, A_IN_PAD - w_in.shape[1]))).astype(BF16)
    w_qidx_p = jnp.pad(w_qidx.reshape(Q_LORA, IDX_HEADS, IDX_DIM),
                       ((0, 0), (0, 0), (0, LANES - IDX_DIM))).reshape(Q_LORA, IDX_HEADS * LANES)
    qlat, qidx, ckv, kidx, tail = _dsa_pre(
        h, g, w_in_p, g_q, g_kv, w_uq.astype(BF16), w_uk.astype(BF16), w_qidx_p.astype(BF16))
    bias = _bias_tiles(rel_bias)
    olat = _dsa_attn(qlat, qidx, tail, ckv, kidx, bias, batch=batch, seq=seq)
    return _dsa_post(olat, h, w_uv.astype(BF16), w_o.astype(BF16))


def _hg_proj_kernel(h_ref, g_ref, w_ref, lb_ref, o_ref, xn_ref, *, blocks_per_part):
    j = pl.program_id(1)

    @pl.when(j == 0)
    def _():
        xn_ref[...] = _rms(h_ref[...], g_ref[...]).astype(BF16)

    part = j // blocks_per_part

    @pl.when(jnp.logical_or(part == 0, part == 3))
    def _():
        o_ref[...] = _silu(_dot(xn_ref[...], w_ref[...]))

    @pl.when(part == 1)
    def _():
        lb = lb_ref[...]
        o_ref[...] = lb + (1.0 - lb) * _sigmoid(_dot(xn_ref[...], w_ref[...]))

    @pl.when(part == 2)
    def _():
        o_ref[...] = _dot(xn_ref[...], w_ref[...])


def _hg_proj(h, g, w_in, lb, tm=512, tn=2048):
    tokens, d = h.shape
    n = w_in.shape[1]
    bpp = d // tn
    kern = functools.partial(_hg_proj_kernel, blocks_per_part=bpp)
    return pl.pallas_call(
        kern,
        grid=(tokens // tm, n // tn),
        in_specs=[
            pl.BlockSpec((tm, d), lambda i, j: (i, 0)),
            pl.BlockSpec((1, d), lambda i, j: (0, 0)),
            pl.BlockSpec((d, tn), lambda i, j: (0, j)),
            pl.BlockSpec((1, tn), lambda i, j: (0, j % bpp)),
        ],
        out_specs=pl.BlockSpec((tm, tn), lambda i, j: (i, j)),
        out_shape=jax.ShapeDtypeStruct((tokens, n), F32),
        scratch_shapes=[pltpu.VMEM((tm, d), BF16)],
        compiler_params=_params("parallel", "arbitrary"),
        name="hgrn2_proj",
    )(h, g.reshape(1, d), w_in, lb.reshape(1, d))


def _hg_rec_kernel(q_ref, f_ref, v_ref, tri_ref, o_ref, a_ref, state_ref, as_ref, ks_ref, vs_ref):
    tt = pl.program_id(1)
    rows, width = q_ref.shape
    dk = width // HG_HEADS
    sub = HG_SUB

    @pl.when(tt == 0)
    def _():
        state_ref[...] = jnp.zeros_like(state_ref)

    log_f = jnp.log(f_ref[...])
    tri = tri_ref[...]
    a_sum = None
    for _ in range(3):
        part = log_f.astype(BF16)
        log_f = log_f - part.astype(F32)
        term = _dot(tri, part)
        a_sum = term if a_sum is None else a_sum + term
    a_ref[...] = a_sum

    def step(j, carry):
        r0 = pl.multiple_of(j * sub, sub)
        groups = sub // SUBLANES
        as_ref[...] = a_ref[pl.ds(r0, sub), :]
        ks_ref[...] = 1.0 - f_ref[pl.ds(r0, sub), :]
        vs_ref[...] = v_ref[pl.ds(r0, sub), :]
        for hd in range(HG_HEADS):
            sl = slice(hd * dk, (hd + 1) * dk)
            a = as_ref[:, sl]
            q = q_ref[pl.ds(r0, sub), sl]
            k = ks_ref[:, sl]
            v = vs_ref[:, sl]
            a_last = a[sub - 1:sub, :]

            intra = [jnp.zeros((SUBLANES, dk), F32) for _ in range(groups)]
            for s in range(sub):
                g0 = s // SUBLANES
                lo = g0 * SUBLANES
                t_idx = lo + lax.broadcasted_iota(jnp.int32, (sub - lo, 1), 0)
                a_s = as_ref[s:s + 1, sl]
                k_s = ks_ref[s:s + 1, sl]
                v_s = vs_ref[s:s + 1, sl]
                e = jnp.exp(jnp.where(t_idx >= s, a[lo:, :] - a_s, -jnp.inf))
                w = q[lo:, :] * (k_s * e)
                upd = jnp.sum(w, axis=-1, keepdims=True) * v_s
                for g in range(g0, groups):
                    rows_g = slice((g - g0) * SUBLANES, (g - g0 + 1) * SUBLANES)
                    intra[g] = intra[g] + upd[rows_g, :]

            st = state_ref[hd]
            q_dec = (q * jnp.exp(a)).astype(BF16)
            k_dec = (k * jnp.exp(a_last - a)).astype(BF16)
            inter = _dot_nt(q_dec, st.astype(BF16))
            o_ref[pl.ds(r0, sub), sl] = inter + jnp.concatenate(intra, axis=0)
            state_ref[hd] = st * jnp.exp(a_last) + _dot_tn(v.astype(BF16), k_dec)
        return carry

    lax.fori_loop(0, rows // sub, step, 0)


def _hg_rec(proj, *, batch, seq, d, tt=256):
    tokens = batch * seq
    tt = min(tt, seq)
    nt = seq // tt
    dk = d // HG_HEADS
    tri = (np.arange(tt)[:, None] >= np.arange(tt)[None, :]) & (
        np.arange(tt)[:, None] // HG_SUB == np.arange(tt)[None, :] // HG_SUB)
    tri = jnp.asarray(tri.astype(np.float32)).astype(BF16)
    return pl.pallas_call(
        _hg_rec_kernel,
        grid=(batch, nt),
        in_specs=[
            pl.BlockSpec((tt, d), lambda b, t: (b * nt + t, 0)),
            pl.BlockSpec((tt, d), lambda b, t: (b * nt + t, 1)),
            pl.BlockSpec((tt, d), lambda b, t: (b * nt + t, 2)),
            _const_spec((tt, tt)),
        ],
        out_specs=pl.BlockSpec((tt, d), lambda b, t: (b * nt + t, 0)),
        out_shape=jax.ShapeDtypeStruct((tokens, d), F32),
        scratch_shapes=[pltpu.VMEM((tt, d), F32), pltpu.VMEM((HG_HEADS, dk, dk), F32)]
        + [pltpu.VMEM((HG_SUB, d), F32)] * 3,
        compiler_params=_params("parallel", "arbitrary"),
        name="hgrn2_recurrence",
    )(proj, proj, proj, tri)


def _hg_post_kernel(o_ref, gate_ref, h_ref, gn_ref, w_ref, out_ref, y_ref):
    width = o_ref.shape[1]
    dv = width // HG_HEADS
    for hd in range(HG_HEADS):
        sl = slice(hd * dv, (hd + 1) * dv)
        o = o_ref[:, sl]
        o = o * lax.rsqrt(jnp.mean(o * o, axis=-1, keepdims=True) + EPS)
        y_ref[:, sl] = (o * gn_ref[:, sl] * gate_ref[:, sl]).astype(BF16)
    out_ref[...] = h_ref[...] + _dot(y_ref[...], w_ref[...])


def _hg_post(o, proj, h, g_norm, w_o, tm=512):
    tokens, d = h.shape
    return pl.pallas_call(
        _hg_post_kernel,
        grid=(tokens // tm,),
        in_specs=[
            pl.BlockSpec((tm, d), lambda i: (i, 0)),
            pl.BlockSpec((tm, d), lambda i: (i, 3)),
            pl.BlockSpec((tm, d), lambda i: (i, 0)),
            _const_spec((1, d)),
            _const_spec((d, d)),
        ],
        out_specs=pl.BlockSpec((tm, d), lambda i: (i, 0)),
        out_shape=jax.ShapeDtypeStruct((tokens, d), F32),
        scratch_shapes=[pltpu.VMEM((tm, d), BF16)],
        compiler_params=_params("parallel"),
        name="hgrn2_post",
    )(o, proj, h, g_norm.reshape(1, d), w_o)


def _hgrn2_layer(h, g, w_in, lb, g_norm, w_o, *, batch, seq):
    d = h.shape[1]
    proj = _hg_proj(h, g, w_in.astype(BF16), lb)
    o = _hg_rec(proj, batch=batch, seq=seq, d=d)
    return _hg_post(o, proj, h, g_norm, w_o.astype(BF16))


def _lower_bounds_kernel(b_ref, o_ref):
    b = b_ref[...]
    e = jnp.exp(b - jnp.max(b, axis=0, keepdims=True))
    soft = e / jnp.sum(e, axis=0, keepdims=True)
    run = jnp.zeros_like(soft[0:1])
    for layer in range(soft.shape[0]):
        if layer > 0:
            run = run + soft[layer:layer + 1]
        o_ref[layer:layer + 1, :] = run


def _lower_bounds(b):
    return pl.pallas_call(
        _lower_bounds_kernel,
        out_shape=jax.ShapeDtypeStruct(b.shape, F32),
        name="hgrn2_lower_bounds",
    )(b)


POOL_HALO = 16


def _pool_kernel(h_ref, halo_ref, g_ref, w_ref, sc_ref, o_ref, *, tiles_per_seq):
    i = pl.program_id(0)
    tm, d = h_ref.shape
    group = d // len(POOL_WINDOWS)
    g = g_ref[...]
    u = _rms(h_ref[...], g)
    hu = _rms(halo_ref[...], g) * _not_first(i, tiles_per_seq)
    ext = jnp.concatenate([hu, u], axis=0)
    pos = (i % tiles_per_seq) * tm + lax.broadcasted_iota(jnp.int32, (tm, group), 0)
    for gi, win in enumerate(POOL_WINDOWS):
        sl = slice(gi * group, (gi + 1) * group)
        s = ext[:, sl]
        span = 1
        while span < win:
            s = s + pltpu.roll(s, span, axis=0)
            span *= 2
        count = jnp.minimum(pos + 1, win).astype(F32)
        diff = s[POOL_HALO:, :] / count - u[:, sl]
        y = _dot(diff.astype(BF16), w_ref[gi])
        o_ref[:, sl] = h_ref[:, sl] + y * sc_ref[:, sl]


def _pool_layer(h, g, w_group, scale, *, seq, tm=256):
    tokens, d = h.shape
    tm = min(tm, seq)
    assert max(POOL_WINDOWS) <= POOL_HALO
    kern = functools.partial(_pool_kernel, tiles_per_seq=seq // tm)
    return pl.pallas_call(
        kern,
        grid=(tokens // tm,),
        in_specs=[
            pl.BlockSpec((tm, d), lambda i: (i, 0)),
            pl.BlockSpec((POOL_HALO, d), lambda i: (jnp.maximum(i * (tm // POOL_HALO) - 1, 0), 0)),
            _const_spec((1, d)),
            _const_spec(w_group.shape),
            _const_spec((1, d)),
        ],
        out_specs=pl.BlockSpec((tm, d), lambda i: (i, 0)),
        out_shape=jax.ShapeDtypeStruct((tokens, d), F32),
        compiler_params=_params("parallel"),
        name="pool_mixer",
    )(h, h, g.reshape(1, d), w_group.astype(BF16), scale.reshape(1, d))


CONF_HALO = 32


def _conf_glu_kernel(h_ref, g_ref, wa_ref, wg_ref, ba_ref, bg_ref, o_ref, xn_ref):
    @pl.when(pl.program_id(1) == 0)
    def _():
        xn_ref[...] = _rms(h_ref[...], g_ref[...]).astype(BF16)

    xn = xn_ref[...]
    a = _dot(xn, wa_ref[...]) + ba_ref[...]
    gate = _dot(xn, wg_ref[...]) + bg_ref[...]
    o_ref[...] = a * _sigmoid(gate)


def _conf_glu(h, g, w_pw1, b_pw1, tm=512, tn=1024):
    tokens, d = h.shape
    nj = d // tn
    b = b_pw1.reshape(1, -1)
    return pl.pallas_call(
        _conf_glu_kernel,
        grid=(tokens // tm, nj),
        in_specs=[
            pl.BlockSpec((tm, d), lambda i, j: (i, 0)),
            pl.BlockSpec((1, d), lambda i, j: (0, 0)),
            pl.BlockSpec((d, tn), lambda i, j: (0, j)),
            pl.BlockSpec((d, tn), lambda i, j: (0, nj + j)),
            pl.BlockSpec((1, tn), lambda i, j: (0, j)),
            pl.BlockSpec((1, tn), lambda i, j: (0, nj + j)),
        ],
        out_specs=pl.BlockSpec((tm, tn), lambda i, j: (i, j)),
        out_shape=jax.ShapeDtypeStruct((tokens, d), F32),
        scratch_shapes=[pltpu.VMEM((tm, d), BF16)],
        compiler_params=_params("parallel", "arbitrary"),
        name="conformer_glu",
    )(h, g.reshape(1, d), w_pw1, w_pw1, b, b)


def _conf_conv_kernel(u_ref, halo_ref, h_ref, wdw_ref, bdw_ref, lng_ref, lnb_ref, w2_ref, b2_ref,
                      o_ref, ext_ref, c_ref, *, tiles_per_seq):
    i = pl.program_id(0)
    tm, d = u_ref.shape
    width = wdw_ref.shape[0]
    ext_ref[:CONF_HALO, :] = halo_ref[...] * _not_first(i, tiles_per_seq)
    ext_ref[CONF_HALO:, :] = u_ref[...]
    base = CONF_HALO - (width - 1)
    n_ext = tm + CONF_HALO
    for c in range(d // LANES):
        sl = slice(c * LANES, (c + 1) * LANES)
        ext = ext_ref[:, sl]
        acc = jnp.broadcast_to(bdw_ref[:, sl], (tm, LANES))
        for phase in range(SUBLANES):
            taps = [j for j in range(width) if (base + j) % SUBLANES == phase]
            if not taps:
                continue
            shifted = ext if phase == 0 else pltpu.roll(ext, n_ext - phase, axis=0)
            for j in taps:
                off = base + j - phase
                acc = acc + wdw_ref[j:j + 1, sl] * shifted[off:off + tm, :]
        c_ref[:, sl] = acc
    x = c_ref[...]
    mu = jnp.mean(x, axis=-1, keepdims=True)
    xc = x - mu
    y = xc * lax.rsqrt(jnp.mean(xc * xc, axis=-1, keepdims=True) + EPS)
    y = _silu(y * lng_ref[...] + lnb_ref[...]).astype(BF16)
    o_ref[...] = h_ref[...] + _dot(y, w2_ref[...]) + b2_ref[...]


def _conf_conv(u, h, w_dw, b_dw, ln_g, ln_b, w_pw2, b_pw2, *, seq, tm=256):
    tokens, d = h.shape
    tm = min(tm, seq)
    width = w_dw.shape[0]
    assert width - 1 <= CONF_HALO
    kern = functools.partial(_conf_conv_kernel, tiles_per_seq=seq // tm)
    return pl.pallas_call(
        kern,
        grid=(tokens // tm,),
        in_specs=[
            pl.BlockSpec((tm, d), lambda i: (i, 0)),
            pl.BlockSpec((CONF_HALO, d), lambda i: (jnp.maximum(i * (tm // CONF_HALO) - 1, 0), 0)),
            pl.BlockSpec((tm, d), lambda i: (i, 0)),
            _const_spec((width, d)),
            _const_spec((1, d)),
            _const_spec((1, d)),
            _const_spec((1, d)),
            _const_spec((d, d)),
            _const_spec((1, d)),
        ],
        out_specs=pl.BlockSpec((tm, d), lambda i: (i, 0)),
        out_shape=jax.ShapeDtypeStruct((tokens, d), F32),
        scratch_shapes=[pltpu.VMEM((tm + CONF_HALO, d), F32), pltpu.VMEM((tm, d), F32)],
        compiler_params=_params("parallel"),
        name="conformer_conv",
    )(u, u, h, w_dw, b_dw.reshape(1, d), ln_g.reshape(1, d), ln_b.reshape(1, d), w_pw2,
      b_pw2.reshape(1, d))


def _conformer_layer(h, g, w_pw1, b_pw1, w_dw, b_dw, ln_g, ln_b, w_pw2, b_pw2, *, seq):
    u = _conf_glu(h, g, w_pw1.astype(BF16), b_pw1)
    return _conf_conv(u, h, w_dw, b_dw, ln_g, ln_b, w_pw2.astype(BF16), b_pw2, seq=seq)


def kernel(x, rel_bias, a_w_in, a_g_q, a_g_kv, a_w_uq, a_w_qidx, a_w_uk, a_w_uv, a_w_o, b_w_in, b_lower_bounds, b_g_norm, b_w_o, c_w_group, c_scale, d_w_pw1, d_b_pw1, d_w_dw, d_b_dw, d_ln_g, d_ln_b, d_w_pw2, d_b_pw2, norm_mix, norm_ffn, ffn_w_up, ffn_w_conv, ffn_b_conv, ffn_w_down, final_norm):
    batch, seq, d = x.shape
    depth = norm_mix.shape[0]
    lb_all = _lower_bounds(b_lower_bounds)
    h = x.reshape(batch * seq, d)
    for i in range(depth):
        j, kind = divmod(i, 4)
        g = norm_mix[i]
        if kind == 0:
            h = _dsa_layer(h, g, rel_bias, a_w_in[j], a_g_q[j], a_g_kv[j], a_w_uq[j], a_w_qidx[j],
                           a_w_uk[j], a_w_uv[j], a_w_o[j], batch=batch, seq=seq)
        elif kind == 1:
            h = _hgrn2_layer(h, g, b_w_in[j], lb_all[i], b_g_norm[j], b_w_o[j], batch=batch, seq=seq)
        elif kind == 2:
            h = _pool_layer(h, g, c_w_group[j], c_scale[j], seq=seq)
        else:
            h = _conformer_layer(h, g, d_w_pw1[j], d_b_pw1[j], d_w_dw[j], d_b_dw[j], d_ln_g[j],
                                 d_ln_b[j], d_w_pw2[j], d_b_pw2[j], seq=seq)
        h = _ffn_layer(h, norm_ffn[i], ffn_w_up[i].astype(BF16), ffn_w_conv[i], ffn_b_conv[i],
                       ffn_w_down[i].astype(BF16), final_norm, seq=seq,
                       final_norm=(i == depth - 1))
    return h.reshape(batch, seq, d)
```

```python
import functools
import math

import jax
import jax.numpy as jnp
import numpy as np
from jax import lax
from jax.experimental import pallas as pl
from jax.experimental.pallas import tpu as pltpu

F32 = jnp.float32
BF16 = jnp.bfloat16

EPS = 1e-6
ATT_HEADS = 16
ATT_HEAD_DIM = 128
Q_LORA = 512
KV_LORA = 256
IDX_HEADS = 16
IDX_DIM = 64
TOPK_MAX = 256
REL_BUCKETS = 32
REL_MAX_DIST = 128
HG_HEADS = 16
POOL_WINDOWS = (2, 4, 8, 16)

LANES = 128
SUBLANES = 8
VMEM_LIMIT = 56 * 1024 * 1024
MASKED = -1e30
INT_MIN = -2 ** 31

A_IN_PAD = 896
DSA_TQ = 128
DSA_TK = 256
DSA_PRE_BLOCKS = 4
HG_SUB = 16
LOG2E = math.log2(math.e)


def _dot(a, b):
    return jnp.dot(a, b, preferred_element_type=F32)


def _dot_nt(a, b):
    return lax.dot_general(a, b, (((1,), (1,)), ((), ())), preferred_element_type=F32)


def _dot_tn(a, b):
    return lax.dot_general(a, b, (((0,), (0,)), ((), ())), preferred_element_type=F32)


def _rms(x, g):
    return x * lax.rsqrt(jnp.mean(x * x, axis=-1, keepdims=True) + EPS) * g


def _sigmoid(x):
    return 1.0 / (1.0 + jnp.exp(-x))


def _silu(x):
    return x * _sigmoid(x)


def _not_first(tile, tiles_per_seq):
    return jnp.where(tile % tiles_per_seq != 0, 1.0, 0.0).astype(F32)


def _params(*sem):
    return pltpu.CompilerParams(dimension_semantics=sem, vmem_limit_bytes=VMEM_LIMIT)


def _const_spec(shape):
    nd = len(shape)
    return pl.BlockSpec(shape, lambda *_: (0,) * nd)


FFN_HALO = 16


def _ffn_kernel(h_ref, halo_ref, g_ref, wa_ref, wb_ref, ca_ref, cb_ref, ba_ref, bb_ref, wd_ref,
                gf_ref, o_ref, xn_ref, *, tiles_per_seq, final_norm):
    i = pl.program_id(0)
    f = pl.program_id(1)

    @pl.when(f == 0)
    def _():
        g = g_ref[...]
        h = h_ref[...]
        xn_ref[FFN_HALO:, :] = _rms(h, g).astype(BF16)
        hn = _rms(halo_ref[...], g) * _not_first(i, tiles_per_seq)
        xn_ref[:FFN_HALO, :] = hn.astype(BF16)
        o_ref[...] = h

    xn = xn_ref[...]
    width = ca_ref.shape[0]

    def conv(u, w_ref, b_ref):
        w = w_ref[...]
        out = b_ref[...] + w[width - 1:width, :] * u[FFN_HALO:, :]
        for j in range(width - 1):
            shift = width - 1 - j
            out = out + w[j:j + 1, :] * pltpu.roll(u, shift, axis=0)[FFN_HALO:, :]
        return out

    a = conv(_dot(xn, wa_ref[...]), ca_ref, ba_ref)
    b = conv(_dot(xn, wb_ref[...]), cb_ref, bb_ref)
    gated = (_silu(a) * b).astype(BF16)
    o_ref[...] += _dot(gated, wd_ref[...])

    if final_norm:
        @pl.when(f == pl.num_programs(1) - 1)
        def _():
            o_ref[...] = _rms(o_ref[...], gf_ref[...])


def _ffn_layer(h, g, w_up, w_conv, b_conv, w_down, gf, *, seq, final_norm, tm=512, tf=512):
    tokens, d = h.shape
    d_ff = w_down.shape[0]
    tm = min(tm, seq)
    nf = d_ff // tf
    width = w_conv.shape[0]
    assert d_ff % tf == 0 and seq % tm == 0 and width - 1 <= FFN_HALO
    kern = functools.partial(_ffn_kernel, tiles_per_seq=seq // tm, final_norm=final_norm)
    return pl.pallas_call(
        kern,
        grid=(tokens // tm, nf),
        in_specs=[
            pl.BlockSpec((tm, d), lambda i, f: (i, 0)),
            pl.BlockSpec((FFN_HALO, d), lambda i, f: (jnp.maximum(i * (tm // FFN_HALO) - 1, 0), 0)),
            pl.BlockSpec((1, d), lambda i, f: (0, 0)),
            pl.BlockSpec((d, tf), lambda i, f: (0, f)),
            pl.BlockSpec((d, tf), lambda i, f: (0, nf + f)),
            pl.BlockSpec((width, tf), lambda i, f: (0, f)),
            pl.BlockSpec((width, tf), lambda i, f: (0, nf + f)),
            pl.BlockSpec((1, tf), lambda i, f: (0, f)),
            pl.BlockSpec((1, tf), lambda i, f: (0, nf + f)),
            pl.BlockSpec((tf, d), lambda i, f: (f, 0)),
            pl.BlockSpec((1, d), lambda i, f: (0, 0)),
        ],
        out_specs=pl.BlockSpec((tm, d), lambda i, f: (i, 0)),
        out_shape=jax.ShapeDtypeStruct((tokens, d), F32),
        scratch_shapes=[pltpu.VMEM((tm + FFN_HALO, d), BF16)],
        compiler_params=_params("parallel", "arbitrary"),
        name="conv_ffn",
    )(h, h, g.reshape(1, d), w_up, w_up, w_conv, w_conv, b_conv.reshape(1, -1),
      b_conv.reshape(1, -1), w_down, gf.reshape(1, d))


def _dsa_pre_kernel(h_ref, g_ref, win_ref, gq_ref, gkv_ref, wuq_ref, wuk_ref, wqi_ref,
                    qlat_ref, qidx_ref, ckv_ref, kidx_ref, tail_ref):
    xn = _rms(h_ref[...], g_ref[...]).astype(BF16)
    p = _dot(xn, win_ref[...])
    cq = _rms(p[:, :Q_LORA], gq_ref[...]).astype(BF16)
    ckv_ref[...] = _rms(p[:, Q_LORA:Q_LORA + KV_LORA], gkv_ref[...]).astype(BF16)
    tail = p[:, Q_LORA + KV_LORA:]
    lane = lax.broadcasted_iota(jnp.int32, tail.shape, 1)
    kidx_ref[...] = jnp.where(lane < IDX_DIM, tail, 0.0).astype(BF16)
    tail_ref[...] = tail
    q = _dot(cq, wuq_ref[...]).astype(BF16)
    qi = _dot(cq, wqi_ref[...])
    q_scale = (ATT_HEAD_DIM ** -0.5) * LOG2E
    for hd in range(ATT_HEADS):
        sl = slice(hd * ATT_HEAD_DIM, (hd + 1) * ATT_HEAD_DIM)
        ql = (_dot(q[:, sl], wuk_ref[hd]) * q_scale).astype(BF16)
        qx = qi[:, sl].astype(BF16)
        for blk in range(DSA_PRE_BLOCKS):
            rows = slice(blk * DSA_TQ, (blk + 1) * DSA_TQ)
            qlat_ref[blk, hd] = ql[rows]
            qidx_ref[blk, hd] = qx[rows]


def _dsa_pre(h, g, w_in, g_q, g_kv, w_uq, w_uk, w_qidx):
    tokens, d = h.shape
    nq = DSA_PRE_BLOCKS
    tm = nq * DSA_TQ
    nblk = tokens // DSA_TQ
    hq = ATT_HEADS * ATT_HEAD_DIM
    return pl.pallas_call(
        _dsa_pre_kernel,
        grid=(tokens // tm,),
        in_specs=[
            pl.BlockSpec((tm, d), lambda i: (i, 0)),
            _const_spec((1, d)),
            _const_spec((d, A_IN_PAD)),
            _const_spec((1, Q_LORA)),
            _const_spec((1, KV_LORA)),
            _const_spec((Q_LORA, hq)),
            _const_spec((ATT_HEADS, ATT_HEAD_DIM, KV_LORA)),
            _const_spec((Q_LORA, IDX_HEADS * LANES)),
        ],
        out_specs=[
            pl.BlockSpec((nq, ATT_HEADS, DSA_TQ, KV_LORA), lambda i: (i, 0, 0, 0)),
            pl.BlockSpec((nq, IDX_HEADS, DSA_TQ, LANES), lambda i: (i, 0, 0, 0)),
            pl.BlockSpec((tm, KV_LORA), lambda i: (i, 0)),
            pl.BlockSpec((tm, LANES), lambda i: (i, 0)),
            pl.BlockSpec((tm, LANES), lambda i: (i, 0)),
        ],
        out_shape=[
            jax.ShapeDtypeStruct((nblk, ATT_HEADS, DSA_TQ, KV_LORA), BF16),
            jax.ShapeDtypeStruct((nblk, IDX_HEADS, DSA_TQ, LANES), BF16),
            jax.ShapeDtypeStruct((tokens, KV_LORA), BF16),
            jax.ShapeDtypeStruct((tokens, LANES), BF16),
            jax.ShapeDtypeStruct((tokens, LANES), F32),
        ],
        compiler_params=_params("parallel"),
        name="dsa_pre",
    )(h, g.reshape(1, d), w_in, g_q.reshape(1, -1), g_kv.reshape(1, -1), w_uq, w_uk, w_qidx)


def _t5_bucket_np(dist):
    max_exact = REL_BUCKETS // 2
    n = np.maximum(dist, 0)
    nf = np.maximum(n, 1).astype(np.float32)
    large = max_exact + (np.log(nf / np.float32(max_exact)) / np.float32(math.log(REL_MAX_DIST / max_exact))
                         * np.float32(REL_BUCKETS - max_exact)).astype(np.int32)
    large = np.minimum(large, REL_BUCKETS - 1)
    return np.where(n < max_exact, n, large).astype(np.int32)


def _bias_tile_buckets():
    i = np.arange(DSA_TQ)[:, None]
    j = np.arange(DSA_TQ)[None, :]
    d0 = _t5_bucket_np(i - j)
    d1 = _t5_bucket_np(DSA_TQ + i - j)
    far = _t5_bucket_np(np.full((DSA_TQ, DSA_TQ), 2 * DSA_TQ + 1))
    assert (far == _t5_bucket_np(np.full((DSA_TQ, DSA_TQ), 10 ** 6))).all()
    kinds = [np.concatenate(p, axis=1) for p in ((far, far), (d0, far), (far, d1), (d1, d0))]
    return np.stack(kinds).astype(np.int32)


def _far_bucket():
    return int(_t5_bucket_np(np.array(2 * DSA_TQ + 1)))


def _bias_expand_kernel(idx_ref, rb_ref, o_ref):
    hd = pl.program_id(1)
    idx = idx_ref[0]
    out = jnp.zeros(idx.shape, F32)
    for b in range(REL_BUCKETS):
        out = jnp.where(idx == b, rb_ref[b, hd], out)
    o_ref[0, 0] = (out - rb_ref[_far_bucket(), hd]) * LOG2E


def _bias_tiles(rel_bias):
    idx = jnp.asarray(_bias_tile_buckets())
    return pl.pallas_call(
        _bias_expand_kernel,
        grid=(4, ATT_HEADS),
        in_specs=[
            pl.BlockSpec((1, DSA_TQ, DSA_TK), lambda k, hd: (k, 0, 0)),
            pl.BlockSpec(memory_space=pltpu.SMEM),
        ],
        out_specs=pl.BlockSpec((1, 1, DSA_TQ, DSA_TK), lambda k, hd: (k, hd, 0, 0)),
        out_shape=jax.ShapeDtypeStruct((4, ATT_HEADS, DSA_TQ, DSA_TK), F32),
        name="dsa_bias_tiles",
    )(idx, rel_bias)


def _sortable(x):
    bits = lax.bitcast_convert_type(x, jnp.int32)
    return jnp.where(bits < 0, bits ^ jnp.int32(0x7FFFFFFF), bits)


def _dsa_attn_kernel(qlat_ref, qidx_ref, tail_ref, ckv_ref, kidx_ref, bias_ref, o_ref,
                     keys_ref, keyst_ref, wb_ref, m_ref, l_ref, alpha_ref, acc_ref, lg_ref, p_ref,
                     *, topk):
    qb = pl.program_id(1)
    tq, tk = DSA_TQ, DSA_TK
    rows = ATT_HEADS * tq
    n_kt = qb // (tk // tq) + 1
    q_pos = qb * tq + lax.broadcasted_iota(jnp.int32, (tq, tk), 0)
    k_off = lax.broadcasted_iota(jnp.int32, (tq, tk), 1)

    tail = tail_ref[...]
    w_scale = (IDX_HEADS ** -0.5) * (IDX_DIM ** -0.5)
    for hd in range(IDX_HEADS):
        col = tail[:, IDX_DIM + hd:IDX_DIM + hd + 1] * w_scale
        wb_ref[hd * tq:(hd + 1) * tq, :] = jnp.broadcast_to(col, (tq, LANES))

    def score_pair(pair, carry):
        for t in range(2):
            kt = 2 * pair + t
            start = pl.multiple_of(kt * tk, tk)
            kk = kidx_ref[pl.ds(start, tk), :]
            logits = jnp.maximum(_dot_nt(qidx_ref[0], kk), 0.0)
            wb = wb_ref[...]
            weighted = jnp.concatenate(
                [logits[:, c * LANES:(c + 1) * LANES] * wb for c in range(tk // LANES)], axis=1)
            score = jnp.sum(weighted.reshape(IDX_HEADS, tq, tk), axis=0)
            score = jnp.where(start + k_off <= q_pos, score, -jnp.inf)
            keys_ref[kt] = _sortable(score)
            keyst_ref[kt] = _sortable(score.T)
        return carry

    lax.fori_loop(0, (n_kt + 1) // 2, score_pair, 0)

    reps = tk // LANES

    def count_ge(cand):
        def body(pair, cnt):
            ge = (keyst_ref[pl.ds(2 * pair, 2)] >= cand).astype(jnp.int32)
            return cnt + jnp.sum(ge.reshape(2 * tk // SUBLANES, SUBLANES, tq), axis=0)
        cnt = lax.fori_loop(0, (n_kt + 1) // 2, body, jnp.zeros((SUBLANES, tq), jnp.int32))
        return jnp.sum(cnt, axis=0, keepdims=True)

    zero = jnp.zeros((1, tq), jnp.int32)
    thr0 = jnp.where(count_ge(zero) >= topk, zero, jnp.full((1, tq), INT_MIN, jnp.int32))

    def bit_step(it, t):
        cand = t | jnp.left_shift(jnp.int32(1), 30 - it)
        return jnp.where(count_ge(cand) >= topk, cand, t)

    thr_row = lax.fori_loop(0, 31, bit_step, thr0)
    thr = lax.bitcast_convert_type(
        jnp.broadcast_to(lax.bitcast_convert_type(thr_row, F32), (LANES, tq)).T, jnp.int32)

    m_ref[...] = jnp.full(m_ref.shape, MASKED, F32)
    l_ref[...] = jnp.zeros_like(l_ref)
    acc_ref[...] = jnp.zeros_like(acc_ref)
    parity = qb % 2

    def head_update(hd, slot):
        r = slice(hd * tq, (hd + 1) * tq)
        lg = lg_ref[slot, r, :]
        m_prev = m_ref[r, :]
        tile_max = jnp.maximum(lg[:, :LANES], lg[:, LANES:2 * LANES])
        for c in range(2, reps):
            tile_max = jnp.maximum(tile_max, lg[:, c * LANES:(c + 1) * LANES])
        m_new = jnp.maximum(m_prev, jnp.max(tile_max.astype(F32), axis=-1, keepdims=True))
        alpha = jnp.exp2(m_prev - m_new)
        p = jnp.exp2(lg - jnp.concatenate([m_new.astype(BF16)] * reps, axis=1))
        tile_sum = p[:, :LANES] + p[:, LANES:2 * LANES]
        for c in range(2, reps):
            tile_sum = tile_sum + p[:, c * LANES:(c + 1) * LANES]
        row_sum = jnp.sum(tile_sum.astype(F32), axis=-1, keepdims=True)
        l_ref[r, :] = alpha * l_ref[r, :] + row_sum
        m_ref[r, :] = m_new
        alpha_ref[slot, r, :] = alpha
        p_ref[slot, r, :] = p

    def attend(kt, near, slot):
        start = pl.multiple_of(kt * tk, tk)
        kv = ckv_ref[pl.ds(start, tk), :]
        ks = keys_ref[kt]
        sel = jnp.concatenate(
            [ks[:, c * LANES:(c + 1) * LANES] >= thr for c in range(reps)], axis=1)
        if near:
            sel = jnp.logical_and(sel, start + k_off <= q_pos)
        shift = jnp.where(sel, 0.0, MASKED)[None]
        if near:
            kind = jnp.where(kt == n_kt - 1, 1 + 2 * parity, 2 * (1 - parity))
            shift = shift + bias_ref[kind]
        logits = _dot_nt(qlat_ref[0], kv).reshape(ATT_HEADS, tq, tk) + shift
        lg_ref[slot] = logits.reshape(rows, tk).astype(BF16)
        for hd in range(ATT_HEADS):
            head_update(hd, slot)
        alpha = alpha_ref[slot]
        acc_ref[...] = (acc_ref[...] * jnp.concatenate([alpha] * (KV_LORA // LANES), axis=1)
                        + _dot(p_ref[slot], kv))

    n_far = jnp.maximum(n_kt - 2, 0)
    n_pairs = n_far // 2

    def far_pair(pair, carry):
        attend(2 * pair, False, 0)
        attend(2 * pair + 1, False, 1)
        return carry

    def far_tile(kt, carry):
        attend(kt, False, 0)
        return carry

    def near_tile(kt, carry):
        attend(kt, True, 0)
        return carry

    lax.fori_loop(0, n_pairs, far_pair, 0)
    lax.fori_loop(2 * n_pairs, n_far, far_tile, 0)
    lax.fori_loop(n_far, n_kt, near_tile, 0)

    inv = 1.0 / l_ref[...]
    out = acc_ref[...] * jnp.concatenate([inv] * (KV_LORA // LANES), axis=1)
    for hd in range(ATT_HEADS):
        o_ref[:, hd * KV_LORA:(hd + 1) * KV_LORA] = out[hd * tq:(hd + 1) * tq, :].astype(BF16)


def _dsa_attn(qlat, qidx, tail, ckv, kidx, bias, *, batch, seq):
    tq, tk = DSA_TQ, DSA_TK
    assert seq % tk == 0 and tq == LANES and tk == 2 * tq
    nqb = seq // tq
    rows = ATT_HEADS * tq
    topk = min(TOPK_MAX, seq // 4)
    tokens = batch * seq
    qlat = qlat.reshape(batch * nqb, rows, KV_LORA)
    qidx = qidx.reshape(batch * nqb, rows, LANES)
    kern = functools.partial(_dsa_attn_kernel, topk=topk)
    return pl.pallas_call(
        kern,
        grid=(batch, nqb),
        in_specs=[
            pl.BlockSpec((1, rows, KV_LORA), lambda b, q: (b * nqb + q, 0, 0)),
            pl.BlockSpec((1, rows, LANES), lambda b, q: (b * nqb + q, 0, 0)),
            pl.BlockSpec((tq, LANES), lambda b, q: (b * nqb + q, 0)),
            pl.BlockSpec((seq, KV_LORA), lambda b, q: (b, 0)),
            pl.BlockSpec((seq, LANES), lambda b, q: (b, 0)),
            _const_spec((4, ATT_HEADS, tq, tk)),
        ],
        out_specs=pl.BlockSpec((tq, ATT_HEADS * KV_LORA), lambda b, q: (b * nqb + q, 0)),
        out_shape=jax.ShapeDtypeStruct((tokens, ATT_HEADS * KV_LORA), BF16),
        scratch_shapes=[
            pltpu.VMEM((seq // tk, tq, tk), jnp.int32),
            pltpu.VMEM((seq // tk, tk, tq), jnp.int32),
            pltpu.VMEM((rows, LANES), F32),
            pltpu.VMEM((rows, LANES), F32),
            pltpu.VMEM((rows, LANES), F32),
            pltpu.VMEM((2, rows, LANES), F32),
            pltpu.VMEM((rows, KV_LORA), F32),
            pltpu.VMEM((2, rows, tk), BF16),
            pltpu.VMEM((2, rows, tk), BF16),
        ],
        compiler_params=_params("parallel", "arbitrary"),
        name="dsa_attn",
    )(qlat, qidx, tail, ckv, kidx, bias)


def _dsa_post_kernel(olat_ref, h_ref, wuv_ref, wo_ref, o_ref, heads_ref):
    for hd in range(ATT_HEADS):
        oh = _dot(olat_ref[:, hd * KV_LORA:(hd + 1) * KV_LORA], wuv_ref[hd])
        heads_ref[:, hd * ATT_HEAD_DIM:(hd + 1) * ATT_HEAD_DIM] = oh.astype(BF16)
    o_ref[...] = h_ref[...] + _dot(heads_ref[...], wo_ref[...])


def _dsa_post(olat, h, w_uv, w_o, tm=512):
    tokens, d = h.shape
    hq = ATT_HEADS * ATT_HEAD_DIM
    return pl.pallas_call(
        _dsa_post_kernel,
        grid=(tokens // tm,),
        in_specs=[
            pl.BlockSpec((tm, ATT_HEADS * KV_LORA), lambda i: (i, 0)),
            pl.BlockSpec((tm, d), lambda i: (i, 0)),
            _const_spec((ATT_HEADS, KV_LORA, ATT_HEAD_DIM)),
            _const_spec((hq, d)),
        ],
        out_specs=pl.BlockSpec((tm, d), lambda i: (i, 0)),
        out_shape=jax.ShapeDtypeStruct((tokens, d), F32),
        scratch_shapes=[pltpu.VMEM((tm, hq), BF16)],
        compiler_params=_params("parallel"),
        name="dsa_post",
    )(olat, h, w_uv, w_o)


def _dsa_layer(h, g, rel_bias, w_in, g_q, g_kv, w_uq, w_qidx, w_uk, w_uv, w_o, *, batch, seq):
    d = h.shape[1]
    w_in_p = jnp.pad(w_in, ((0, 0), (0, A_IN_PAD - w_in.shape[1]))).astype(BF16)
    w_qidx_p = jnp.pad(w_qidx.reshape(Q_LORA, IDX_HEADS, IDX_DIM),
                       ((0, 0), (0, 0), (0, LANES - IDX_DIM))).reshape(Q_LORA, IDX_HEADS * LANES)
    qlat, qidx, ckv, kidx, tail = _dsa_pre(
        h, g, w_in_p, g_q, g_kv, w_uq.astype(BF16), w_uk.astype(BF16), w_qidx_p.astype(BF16))
    bias = _bias_tiles(rel_bias)
    olat = _dsa_attn(qlat, qidx, tail, ckv, kidx, bias, batch=batch, seq=seq)
    return _dsa_post(olat, h, w_uv.astype(BF16), w_o.astype(BF16))


def _hg_proj_kernel(h_ref, g_ref, w_ref, lb_ref, o_ref, xn_ref, *, blocks_per_part):
    j = pl.program_id(1)

    @pl.when(j == 0)
    def _():
        xn_ref[...] = _rms(h_ref[...], g_ref[...]).astype(BF16)

    part = j // blocks_per_part

    @pl.when(jnp.logical_or(part == 0, part == 3))
    def _():
        o_ref[...] = _silu(_dot(xn_ref[...], w_ref[...]))

    @pl.when(part == 1)
    def _():
        lb = lb_ref[...]
        o_ref[...] = lb + (1.0 - lb) * _sigmoid(_dot(xn_ref[...], w_ref[...]))

    @pl.when(part == 2)
    def _():
        o_ref[...] = _dot(xn_ref[...], w_ref[...])


def _hg_proj(h, g, w_in, lb, tm=512, tn=2048):
    tokens, d = h.shape
    n = w_in.shape[1]
    bpp = d // tn
    kern = functools.partial(_hg_proj_kernel, blocks_per_part=bpp)
    return pl.pallas_call(
        kern,
        grid=(tokens // tm, n // tn),
        in_specs=[
            pl.BlockSpec((tm, d), lambda i, j: (i, 0)),
            pl.BlockSpec((1, d), lambda i, j: (0, 0)),
            pl.BlockSpec((d, tn), lambda i, j: (0, j)),
            pl.BlockSpec((1, tn), lambda i, j: (0, j % bpp)),
        ],
        out_specs=pl.BlockSpec((tm, tn), lambda i, j: (i, j)),
        out_shape=jax.ShapeDtypeStruct((tokens, n), F32),
        scratch_shapes=[pltpu.VMEM((tm, d), BF16)],
        compiler_params=_params("parallel", "arbitrary"),
        name="hgrn2_proj",
    )(h, g.reshape(1, d), w_in, lb.reshape(1, d))


def _hg_rec_kernel(q_ref, f_ref, v_ref, tri_ref, o_ref, a_ref, state_ref, as_ref, ks_ref, vs_ref):
    tt = pl.program_id(1)
    rows, width = q_ref.shape
    dk = width // HG_HEADS
    sub = HG_SUB

    @pl.when(tt == 0)
    def _():
        state_ref[...] = jnp.zeros_like(state_ref)

    log_f = jnp.log(f_ref[...])
    tri = tri_ref[...]
    a_sum = None
    for _ in range(3):
        part = log_f.astype(BF16)
        log_f = log_f - part.astype(F32)
        term = _dot(tri, part)
        a_sum = term if a_sum is None else a_sum + term
    a_ref[...] = a_sum

    def step(j, carry):
        r0 = pl.multiple_of(j * sub, sub)
        groups = sub // SUBLANES
        as_ref[...] = a_ref[pl.ds(r0, sub), :]
        ks_ref[...] = 1.0 - f_ref[pl.ds(r0, sub), :]
        vs_ref[...] = v_ref[pl.ds(r0, sub), :]
        for hd in range(HG_HEADS):
            sl = slice(hd * dk, (hd + 1) * dk)
            a = as_ref[:, sl]
            q = q_ref[pl.ds(r0, sub), sl]
            k = ks_ref[:, sl]
            v = vs_ref[:, sl]
            a_last = a[sub - 1:sub, :]

            intra = [jnp.zeros((SUBLANES, dk), F32) for _ in range(groups)]
            for s in range(sub):
                g0 = s // SUBLANES
                lo = g0 * SUBLANES
                t_idx = lo + lax.broadcasted_iota(jnp.int32, (sub - lo, 1), 0)
                a_s = as_ref[s:s + 1, sl]
                k_s = ks_ref[s:s + 1, sl]
                v_s = vs_ref[s:s + 1, sl]
                e = jnp.exp(jnp.where(t_idx >= s, a[lo:, :] - a_s, -jnp.inf))
                w = q[lo:, :] * (k_s * e)
                upd = jnp.sum(w, axis=-1, keepdims=True) * v_s
                for g in range(g0, groups):
                    rows_g = slice((g - g0) * SUBLANES, (g - g0 + 1) * SUBLANES)
                    intra[g] = intra[g] + upd[rows_g, :]

            st = state_ref[hd]
            q_dec = (q * jnp.exp(a)).astype(BF16)
            k_dec = (k * jnp.exp(a_last - a)).astype(BF16)
            inter = _dot_nt(q_dec, st.astype(BF16))
            o_ref[pl.ds(r0, sub), sl] = inter + jnp.concatenate(intra, axis=0)
            state_ref[hd] = st * jnp.exp(a_last) + _dot_tn(v.astype(BF16), k_dec)
        return carry

    lax.fori_loop(0, rows // sub, step, 0)


def _hg_rec(proj, *, batch, seq, d, tt=256):
    tokens = batch * seq
    tt = min(tt, seq)
    nt = seq // tt
    dk = d // HG_HEADS
    tri = (np.arange(tt)[:, None] >= np.arange(tt)[None, :]) & (
        np.arange(tt)[:, None] // HG_SUB == np.arange(tt)[None, :] // HG_SUB)
    tri = jnp.asarray(tri.astype(np.float32)).astype(BF16)
    return pl.pallas_call(
        _hg_rec_kernel,
        grid=(batch, nt),
        in_specs=[
            pl.BlockSpec((tt, d), lambda b, t: (b * nt + t, 0)),
            pl.BlockSpec((tt, d), lambda b, t: (b * nt + t, 1)),
            pl.BlockSpec((tt, d), lambda b, t: (b * nt + t, 2)),
            _const_spec((tt, tt)),
        ],
        out_specs=pl.BlockSpec((tt, d), lambda b, t: (b * nt + t, 0)),
        out_shape=jax.ShapeDtypeStruct((tokens, d), F32),
        scratch_shapes=[pltpu.VMEM((tt, d), F32), pltpu.VMEM((HG_HEADS, dk, dk), F32)]
        + [pltpu.VMEM((HG_SUB, d), F32)] * 3,
        compiler_params=_params("parallel", "arbitrary"),
        name="hgrn2_recurrence",
    )(proj, proj, proj, tri)


def _hg_post_kernel(o_ref, gate_ref, h_ref, gn_ref, w_ref, out_ref, y_ref):
    width = o_ref.shape[1]
    dv = width // HG_HEADS
    for hd in range(HG_HEADS):
        sl = slice(hd * dv, (hd + 1) * dv)
        o = o_ref[:, sl]
        o = o * lax.rsqrt(jnp.mean(o * o, axis=-1, keepdims=True) + EPS)
        y_ref[:, sl] = (o * gn_ref[:, sl] * gate_ref[:, sl]).astype(BF16)
    out_ref[...] = h_ref[...] + _dot(y_ref[...], w_ref[...])


def _hg_post(o, proj, h, g_norm, w_o, tm=512):
    tokens, d = h.shape
    return pl.pallas_call(
        _hg_post_kernel,
        grid=(tokens // tm,),
        in_specs=[
            pl.BlockSpec((tm, d), lambda i: (i, 0)),
            pl.BlockSpec((tm, d), lambda i: (i, 3)),
            pl.BlockSpec((tm, d), lambda i: (i, 0)),
            _const_spec((1, d)),
            _const_spec((d, d)),
        ],
        out_specs=pl.BlockSpec((tm, d), lambda i: (i, 0)),
        out_shape=jax.ShapeDtypeStruct((tokens, d), F32),
        scratch_shapes=[pltpu.VMEM((tm, d), BF16)],
        compiler_params=_params("parallel"),
        name="hgrn2_post",
    )(o, proj, h, g_norm.reshape(1, d), w_o)


def _hgrn2_layer(h, g, w_in, lb, g_norm, w_o, *, batch, seq):
    d = h.shape[1]
    proj = _hg_proj(h, g, w_in.astype(BF16), lb)
    o = _hg_rec(proj, batch=batch, seq=seq, d=d)
    return _hg_post(o, proj, h, g_norm, w_o.astype(BF16))


def _lower_bounds_kernel(b_ref, o_ref):
    b = b_ref[...]
    e = jnp.exp(b - jnp.max(b, axis=0, keepdims=True))
    soft = e / jnp.sum(e, axis=0, keepdims=True)
    run = jnp.zeros_like(soft[0:1])
    for layer in range(soft.shape[0]):
        if layer > 0:
            run = run + soft[layer:layer + 1]
        o_ref[layer:layer + 1, :] = run


def _lower_bounds(b):
    return pl.pallas_call(
        _lower_bounds_kernel,
        out_shape=jax.ShapeDtypeStruct(b.shape, F32),
        name="hgrn2_lower_bounds",
    )(b)


POOL_HALO = 16


def _pool_kernel(h_ref, halo_ref, g_ref, w_ref, sc_ref, o_ref, *, tiles_per_seq):
    i = pl.program_id(0)
    tm, d = h_ref.shape
    group = d // len(POOL_WINDOWS)
    g = g_ref[...]
    u = _rms(h_ref[...], g)
    hu = _rms(halo_ref[...], g) * _not_first(i, tiles_per_seq)
    ext = jnp.concatenate([hu, u], axis=0)
    pos = (i % tiles_per_seq) * tm + lax.broadcasted_iota(jnp.int32, (tm, group), 0)
    for gi, win in enumerate(POOL_WINDOWS):
        sl = slice(gi * group, (gi + 1) * group)
        s = ext[:, sl]
        span = 1
        while span < win:
            s = s + pltpu.roll(s, span, axis=0)
            span *= 2
        count = jnp.minimum(pos + 1, win).astype(F32)
        diff = s[POOL_HALO:, :] / count - u[:, sl]
        y = _dot(diff.astype(BF16), w_ref[gi])
        o_ref[:, sl] = h_ref[:, sl] + y * sc_ref[:, sl]


def _pool_layer(h, g, w_group, scale, *, seq, tm=256):
    tokens, d = h.shape
    tm = min(tm, seq)
    assert max(POOL_WINDOWS) <= POOL_HALO
    kern = functools.partial(_pool_kernel, tiles_per_seq=seq // tm)
    return pl.pallas_call(
        kern,
        grid=(tokens // tm,),
        in_specs=[
            pl.BlockSpec((tm, d), lambda i: (i, 0)),
            pl.BlockSpec((POOL_HALO, d), lambda i: (jnp.maximum(i * (tm // POOL_HALO) - 1, 0), 0)),
            _const_spec((1, d)),
            _const_spec(w_group.shape),
            _const_spec((1, d)),
        ],
        out_specs=pl.BlockSpec((tm, d), lambda i: (i, 0)),
        out_shape=jax.ShapeDtypeStruct((tokens, d), F32),
        compiler_params=_params("parallel"),
        name="pool_mixer",
    )(h, h, g.reshape(1, d), w_group.astype(BF16), scale.reshape(1, d))


CONF_HALO = 32


def _conf_glu_kernel(h_ref, g_ref, wa_ref, wg_ref, ba_ref, bg_ref, o_ref, xn_ref):
    @pl.when(pl.program_id(1) == 0)
    def _():
        xn_ref[...] = _rms(h_ref[...], g_ref[...]).astype(BF16)

    xn = xn_ref[...]
    a = _dot(xn, wa_ref[...]) + ba_ref[...]
    gate = _dot(xn, wg_ref[...]) + bg_ref[...]
    o_ref[...] = a * _sigmoid(gate)


def _conf_glu(h, g, w_pw1, b_pw1, tm=512, tn=1024):
    tokens, d = h.shape
    nj = d // tn
    b = b_pw1.reshape(1, -1)
    return pl.pallas_call(
        _conf_glu_kernel,
        grid=(tokens // tm, nj),
        in_specs=[
            pl.BlockSpec((tm, d), lambda i, j: (i, 0)),
            pl.BlockSpec((1, d), lambda i, j: (0, 0)),
            pl.BlockSpec((d, tn), lambda i, j: (0, j)),
            pl.BlockSpec((d, tn), lambda i, j: (0, nj + j)),
            pl.BlockSpec((1, tn), lambda i, j: (0, j)),
            pl.BlockSpec((1, tn), lambda i, j: (0, nj + j)),
        ],
        out_specs=pl.BlockSpec((tm, tn), lambda i, j: (i, j)),
        out_shape=jax.ShapeDtypeStruct((tokens, d), F32),
        scratch_shapes=[pltpu.VMEM((tm, d), BF16)],
        compiler_params=_params("parallel", "arbitrary"),
        name="conformer_glu",
    )(h, g.reshape(1, d), w_pw1, w_pw1, b, b)


def _conf_conv_kernel(u_ref, halo_ref, h_ref, wdw_ref, bdw_ref, lng_ref, lnb_ref, w2_ref, b2_ref,
                      o_ref, ext_ref, c_ref, *, tiles_per_seq):
    i = pl.program_id(0)
    tm, d = u_ref.shape
    width = wdw_ref.shape[0]
    ext_ref[:CONF_HALO, :] = halo_ref[...] * _not_first(i, tiles_per_seq)
    ext_ref[CONF_HALO:, :] = u_ref[...]
    base = CONF_HALO - (width - 1)
    n_ext = tm + CONF_HALO
    for c in range(d // LANES):
        sl = slice(c * LANES, (c + 1) * LANES)
        ext = ext_ref[:, sl]
        acc = jnp.broadcast_to(bdw_ref[:, sl], (tm, LANES))
        for phase in range(SUBLANES):
            taps = [j for j in range(width) if (base + j) % SUBLANES == phase]
            if not taps:
                continue
            shifted = ext if phase == 0 else pltpu.roll(ext, n_ext - phase, axis=0)
            for j in taps:
                off = base + j - phase
                acc = acc + wdw_ref[j:j + 1, sl] * shifted[off:off + tm, :]
        c_ref[:, sl] = acc
    x = c_ref[...]
    mu = jnp.mean(x, axis=-1, keepdims=True)
    xc = x - mu
    y = xc * lax.rsqrt(jnp.mean(xc * xc, axis=-1, keepdims=True) + EPS)
    y = _silu(y * lng_ref[...] + lnb_ref[...]).astype(BF16)
    o_ref[...] = h_ref[...] + _dot(y, w2_ref[...]) + b2_ref[...]


def _conf_conv(u, h, w_dw, b_dw, ln_g, ln_b, w_pw2, b_pw2, *, seq, tm=256):
    tokens, d = h.shape
    tm = min(tm, seq)
    width = w_dw.shape[0]
    assert width - 1 <= CONF_HALO
    kern = functools.partial(_conf_conv_kernel, tiles_per_seq=seq // tm)
    return pl.pallas_call(
        kern,
        grid=(tokens // tm,),
        in_specs=[
            pl.BlockSpec((tm, d), lambda i: (i, 0)),
            pl.BlockSpec((CONF_HALO, d), lambda i: (jnp.maximum(i * (tm // CONF_HALO) - 1, 0), 0)),
            pl.BlockSpec((tm, d), lambda i: (i, 0)),
            _const_spec((width, d)),
            _const_spec((1, d)),
            _const_spec((1, d)),
            _const_spec((1, d)),
            _const_spec((d, d)),
            _const_spec((1, d)),
        ],
        out_specs=pl.BlockSpec((tm, d), lambda i: (i, 0)),
        out_shape=jax.ShapeDtypeStruct((tokens, d), F32),
        scratch_shapes=[pltpu.VMEM((tm + CONF_HALO, d), F32), pltpu.VMEM((tm, d), F32)],
        compiler_params=_params("parallel"),
        name="conformer_conv",
    )(u, u, h, w_dw, b_dw.reshape(1, d), ln_g.reshape(1, d), ln_b.reshape(1, d), w_pw2,
      b_pw2.reshape(1, d))


def _conformer_layer(h, g, w_pw1, b_pw1, w_dw, b_dw, ln_g, ln_b, w_pw2, b_pw2, *, seq):
    u = _conf_glu(h, g, w_pw1.astype(BF16), b_pw1)
    return _conf_conv(u, h, w_dw, b_dw, ln_g, ln_b, w_pw2.astype(BF16), b_pw2, seq=seq)


def kernel(x, rel_bias, a_w_in, a_g_q, a_g_kv, a_w_uq, a_w_qidx, a_w_uk, a_w_uv, a_w_o, b_w_in, b_lower_bounds, b_g_norm, b_w_o, c_w_group, c_scale, d_w_pw1, d_b_pw1, d_w_dw, d_b_dw, d_ln_g, d_ln_b, d_w_pw2, d_b_pw2, norm_mix, norm_ffn, ffn_w_up, ffn_w_conv, ffn_b_conv, ffn_w_down, final_norm):
    batch, seq, d = x.shape
    depth = norm_mix.shape[0]
    lb_all = _lower_bounds(b_lower_bounds)
    h = x.reshape(batch * seq, d)
    for i in range(depth):
        j, kind = divmod(i, 4)
        g = norm_mix[i]
        if kind == 0:
            h = _dsa_layer(h, g, rel_bias, a_w_in[j], a_g_q[j], a_g_kv[j], a_w_uq[j], a_w_qidx[j],
                           a_w_uk[j], a_w_uv[j], a_w_o[j], batch=batch, seq=seq)
        elif kind == 1:
            h = _hgrn2_layer(h, g, b_w_in[j], lb_all[i], b_g_norm[j], b_w_o[j], batch=batch, seq=seq)
        elif kind == 2:
            h = _pool_layer(h, g, c_w_group[j], c_scale[j], seq=seq)
        else:
            h = _conformer_layer(h, g, d_w_pw1[j], d_b_pw1[j], d_w_dw[j], d_b_dw[j], d_ln_g[j],
                                 d_ln_b[j], d_w_pw2[j], d_b_pw2[j], seq=seq)
        h = _ffn_layer(h, norm_ffn[i], ffn_w_up[i].astype(BF16), ffn_w_conv[i], ffn_b_conv[i],
                       ffn_w_down[i].astype(BF16), final_norm, seq=seq,
                       final_norm=(i == depth - 1))
    return h.reshape(batch, seq, d)
```

```python
import functools
import math

import jax
import jax.numpy as jnp
import numpy as np
from jax import lax
from jax.experimental import pallas as pl
from jax.experimental.pallas import tpu as pltpu

F32 = jnp.float32
BF16 = jnp.bfloat16

EPS = 1e-6
ATT_HEADS = 16
ATT_HEAD_DIM = 128
Q_LORA = 512
KV_LORA = 256
IDX_HEADS = 16
IDX_DIM = 64
TOPK_MAX = 256
REL_BUCKETS = 32
REL_MAX_DIST = 128
HG_HEADS = 16
POOL_WINDOWS = (2, 4, 8, 16)

LANES = 128
SUBLANES = 8
VMEM_LIMIT = 56 * 1024 * 1024
MASKED = -1e30
INT_MIN = -2 ** 31

A_IN_PAD = 896
DSA_TQ = 128
DSA_TK = 256
DSA_PRE_BLOCKS = 4
HG_SUB = 16
LOG2E = math.log2(math.e)


def _dot(a, b):
    return jnp.dot(a, b, preferred_element_type=F32)


def _dot_nt(a, b):
    return lax.dot_general(a, b, (((1,), (1,)), ((), ())), preferred_element_type=F32)


def _dot_tn(a, b):
    return lax.dot_general(a, b, (((0,), (0,)), ((), ())), preferred_element_type=F32)


def _rms(x, g):
    return x * lax.rsqrt(jnp.mean(x * x, axis=-1, keepdims=True) + EPS) * g


def _sigmoid(x):
    return 1.0 / (1.0 + jnp.exp(-x))


def _silu(x):
    return x * _sigmoid(x)


def _not_first(tile, tiles_per_seq):
    return jnp.where(tile % tiles_per_seq != 0, 1.0, 0.0).astype(F32)


def _params(*sem):
    return pltpu.CompilerParams(dimension_semantics=sem, vmem_limit_bytes=VMEM_LIMIT)


def _const_spec(shape):
    nd = len(shape)
    return pl.BlockSpec(shape, lambda *_: (0,) * nd)


FFN_HALO = 16


def _ffn_kernel(h_ref, halo_ref, g_ref, wa_ref, wb_ref, ca_ref, cb_ref, ba_ref, bb_ref, wd_ref,
                gf_ref, o_ref, xn_ref, *, tiles_per_seq, final_norm):
    i = pl.program_id(0)
    f = pl.program_id(1)

    @pl.when(f == 0)
    def _():
        g = g_ref[...]
        h = h_ref[...]
        xn_ref[FFN_HALO:, :] = _rms(h, g).astype(BF16)
        hn = _rms(halo_ref[...], g) * _not_first(i, tiles_per_seq)
        xn_ref[:FFN_HALO, :] = hn.astype(BF16)
        o_ref[...] = h

    xn = xn_ref[...]
    width = ca_ref.shape[0]

    def conv(u, w_ref, b_ref):
        w = w_ref[...]
        out = b_ref[...] + w[width - 1:width, :] * u[FFN_HALO:, :]
        for j in range(width - 1):
            shift = width - 1 - j
            out = out + w[j:j + 1, :] * pltpu.roll(u, shift, axis=0)[FFN_HALO:, :]
        return out

    a = conv(_dot(xn, wa_ref[...]), ca_ref, ba_ref)
    b = conv(_dot(xn, wb_ref[...]), cb_ref, bb_ref)
    gated = (_silu(a) * b).astype(BF16)
    o_ref[...] += _dot(gated, wd_ref[...])

    if final_norm:
        @pl.when(f == pl.num_programs(1) - 1)
        def _():
            o_ref[...] = _rms(o_ref[...], gf_ref[...])


def _ffn_layer(h, g, w_up, w_conv, b_conv, w_down, gf, *, seq, final_norm, tm=1024, tf=512):
    tokens, d = h.shape
    d_ff = w_down.shape[0]
    tm = min(tm, seq)
    nf = d_ff // tf
    width = w_conv.shape[0]
    assert d_ff % tf == 0 and seq % tm == 0 and width - 1 <= FFN_HALO
    kern = functools.partial(_ffn_kernel, tiles_per_seq=seq // tm, final_norm=final_norm)
    return pl.pallas_call(
        kern,
        grid=(tokens // tm, nf),
        in_specs=[
            pl.BlockSpec((tm, d), lambda i, f: (i, 0)),
            pl.BlockSpec((FFN_HALO, d), lambda i, f: (jnp.maximum(i * (tm // FFN_HALO) - 1, 0), 0)),
            pl.BlockSpec((1, d), lambda i, f: (0, 0)),
            pl.BlockSpec((d, tf), lambda i, f: (0, f)),
            pl.BlockSpec((d, tf), lambda i, f: (0, nf + f)),
            pl.BlockSpec((width, tf), lambda i, f: (0, f)),
            pl.BlockSpec((width, tf), lambda i, f: (0, nf + f)),
            pl.BlockSpec((1, tf), lambda i, f: (0, f)),
            pl.BlockSpec((1, tf), lambda i, f: (0, nf + f)),
            pl.BlockSpec((tf, d), lambda i, f: (f, 0)),
            pl.BlockSpec((1, d), lambda i, f: (0, 0)),
        ],
        out_specs=pl.BlockSpec((tm, d), lambda i, f: (i, 0)),
        out_shape=jax.ShapeDtypeStruct((tokens, d), F32),
        scratch_shapes=[pltpu.VMEM((tm + FFN_HALO, d), BF16)],
        compiler_params=_params("parallel", "arbitrary"),
        name="conv_ffn",
    )(h, h, g.reshape(1, d), w_up, w_up, w_conv, w_conv, b_conv.reshape(1, -1),
      b_conv.reshape(1, -1), w_down, gf.reshape(1, d))


def _dsa_pre_kernel(h_ref, g_ref, win_ref, gq_ref, gkv_ref, wuq_ref, wuk_ref, wqi_ref,
                    qlat_ref, qidx_ref, ckv_ref, kidx_ref, tail_ref):
    xn = _rms(h_ref[...], g_ref[...]).astype(BF16)
    p = _dot(xn, win_ref[...])
    cq = _rms(p[:, :Q_LORA], gq_ref[...]).astype(BF16)
    ckv_ref[...] = _rms(p[:, Q_LORA:Q_LORA + KV_LORA], gkv_ref[...]).astype(BF16)
    tail = p[:, Q_LORA + KV_LORA:]
    lane = lax.broadcasted_iota(jnp.int32, tail.shape, 1)
    kidx_ref[...] = jnp.where(lane < IDX_DIM, tail, 0.0).astype(BF16)
    tail_ref[...] = tail
    q = _dot(cq, wuq_ref[...]).astype(BF16)
    qi = _dot(cq, wqi_ref[...])
    q_scale = (ATT_HEAD_DIM ** -0.5) * LOG2E
    for hd in range(ATT_HEADS):
        sl = slice(hd * ATT_HEAD_DIM, (hd + 1) * ATT_HEAD_DIM)
        ql = (_dot(q[:, sl], wuk_ref[hd]) * q_scale).astype(BF16)
        qx = qi[:, sl].astype(BF16)
        for blk in range(DSA_PRE_BLOCKS):
            rows = slice(blk * DSA_TQ, (blk + 1) * DSA_TQ)
            qlat_ref[blk, hd] = ql[rows]
            qidx_ref[blk, hd] = qx[rows]


def _dsa_pre(h, g, w_in, g_q, g_kv, w_uq, w_uk, w_qidx):
    tokens, d = h.shape
    nq = DSA_PRE_BLOCKS
    tm = nq * DSA_TQ
    nblk = tokens // DSA_TQ
    hq = ATT_HEADS * ATT_HEAD_DIM
    return pl.pallas_call(
        _dsa_pre_kernel,
        grid=(tokens // tm,),
        in_specs=[
            pl.BlockSpec((tm, d), lambda i: (i, 0)),
            _const_spec((1, d)),
            _const_spec((d, A_IN_PAD)),
            _const_spec((1, Q_LORA)),
            _const_spec((1, KV_LORA)),
            _const_spec((Q_LORA, hq)),
            _const_spec((ATT_HEADS, ATT_HEAD_DIM, KV_LORA)),
            _const_spec((Q_LORA, IDX_HEADS * LANES)),
        ],
        out_specs=[
            pl.BlockSpec((nq, ATT_HEADS, DSA_TQ, KV_LORA), lambda i: (i, 0, 0, 0)),
            pl.BlockSpec((nq, IDX_HEADS, DSA_TQ, LANES), lambda i: (i, 0, 0, 0)),
            pl.BlockSpec((tm, KV_LORA), lambda i: (i, 0)),
            pl.BlockSpec((tm, LANES), lambda i: (i, 0)),
            pl.BlockSpec((tm, LANES), lambda i: (i, 0)),
        ],
        out_shape=[
            jax.ShapeDtypeStruct((nblk, ATT_HEADS, DSA_TQ, KV_LORA), BF16),
            jax.ShapeDtypeStruct((nblk, IDX_HEADS, DSA_TQ, LANES), BF16),
            jax.ShapeDtypeStruct((tokens, KV_LORA), BF16),
            jax.ShapeDtypeStruct((tokens, LANES), BF16),
            jax.ShapeDtypeStruct((tokens, LANES), F32),
        ],
        compiler_params=_params("parallel"),
        name="dsa_pre",
    )(h, g.reshape(1, d), w_in, g_q.reshape(1, -1), g_kv.reshape(1, -1), w_uq, w_uk, w_qidx)


def _t5_bucket_np(dist):
    max_exact = REL_BUCKETS // 2
    n = np.maximum(dist, 0)
    nf = np.maximum(n, 1).astype(np.float32)
    large = max_exact + (np.log(nf / np.float32(max_exact)) / np.float32(math.log(REL_MAX_DIST / max_exact))
                         * np.float32(REL_BUCKETS - max_exact)).astype(np.int32)
    large = np.minimum(large, REL_BUCKETS - 1)
    return np.where(n < max_exact, n, large).astype(np.int32)


def _bias_tile_buckets():
    i = np.arange(DSA_TQ)[:, None]
    j = np.arange(DSA_TQ)[None, :]
    d0 = _t5_bucket_np(i - j)
    d1 = _t5_bucket_np(DSA_TQ + i - j)
    far = _t5_bucket_np(np.full((DSA_TQ, DSA_TQ), 2 * DSA_TQ + 1))
    assert (far == _t5_bucket_np(np.full((DSA_TQ, DSA_TQ), 10 ** 6))).all()
    kinds = [np.concatenate(p, axis=1) for p in ((far, far), (d0, far), (far, d1), (d1, d0))]
    return np.stack(kinds).astype(np.int32)


def _far_bucket():
    return int(_t5_bucket_np(np.array(2 * DSA_TQ + 1)))


def _bias_expand_kernel(idx_ref, rb_ref, o_ref):
    hd = pl.program_id(1)
    idx = idx_ref[0]
    out = jnp.zeros(idx.shape, F32)
    for b in range(REL_BUCKETS):
        out = jnp.where(idx == b, rb_ref[b, hd], out)
    o_ref[0, 0] = (out - rb_ref[_far_bucket(), hd]) * LOG2E


def _bias_tiles(rel_bias):
    idx = jnp.asarray(_bias_tile_buckets())
    return pl.pallas_call(
        _bias_expand_kernel,
        grid=(4, ATT_HEADS),
        in_specs=[
            pl.BlockSpec((1, DSA_TQ, DSA_TK), lambda k, hd: (k, 0, 0)),
            pl.BlockSpec(memory_space=pltpu.SMEM),
        ],
        out_specs=pl.BlockSpec((1, 1, DSA_TQ, DSA_TK), lambda k, hd: (k, hd, 0, 0)),
        out_shape=jax.ShapeDtypeStruct((4, ATT_HEADS, DSA_TQ, DSA_TK), F32),
        name="dsa_bias_tiles",
    )(idx, rel_bias)


def _sortable(x):
    bits = lax.bitcast_convert_type(x, jnp.int32)
    return jnp.where(bits < 0, bits ^ jnp.int32(0x7FFFFFFF), bits)


def _dsa_attn_kernel(qlat_ref, qidx_ref, tail_ref, ckv_ref, kidx_ref, bias_ref, o_ref,
                     keys_ref, keyst_ref, wb_ref, m_ref, l_ref, alpha_ref, acc_ref, lg_ref, p_ref,
                     *, topk):
    qb = pl.program_id(1)
    tq, tk = DSA_TQ, DSA_TK
    rows = ATT_HEADS * tq
    n_kt = qb // (tk // tq) + 1
    q_pos = qb * tq + lax.broadcasted_iota(jnp.int32, (tq, tk), 0)
    k_off = lax.broadcasted_iota(jnp.int32, (tq, tk), 1)

    tail = tail_ref[...]
    w_scale = (IDX_HEADS ** -0.5) * (IDX_DIM ** -0.5)
    for hd in range(IDX_HEADS):
        col = tail[:, IDX_DIM + hd:IDX_DIM + hd + 1] * w_scale
        wb_ref[hd * tq:(hd + 1) * tq, :] = jnp.broadcast_to(col, (tq, LANES))

    def score_pair(pair, carry):
        for t in range(2):
            kt = 2 * pair + t
            start = pl.multiple_of(kt * tk, tk)
            kk = kidx_ref[pl.ds(start, tk), :]
            logits = jnp.maximum(_dot_nt(qidx_ref[0], kk), 0.0)
            wb = wb_ref[...]
            weighted = jnp.concatenate(
                [logits[:, c * LANES:(c + 1) * LANES] * wb for c in range(tk // LANES)], axis=1)
            score = jnp.sum(weighted.reshape(IDX_HEADS, tq, tk), axis=0)
            score = jnp.where(start + k_off <= q_pos, score, -jnp.inf)
            keys_ref[kt] = _sortable(score)
            keyst_ref[kt] = _sortable(score.T)
        return carry

    lax.fori_loop(0, (n_kt + 1) // 2, score_pair, 0)

    reps = tk // LANES

    def count_ge(cand):
        def body(pair, cnt):
            ge = (keyst_ref[pl.ds(2 * pair, 2)] >= cand).astype(jnp.int32)
            return cnt + jnp.sum(ge.reshape(2 * tk // SUBLANES, SUBLANES, tq), axis=0)
        cnt = lax.fori_loop(0, (n_kt + 1) // 2, body, jnp.zeros((SUBLANES, tq), jnp.int32))
        return jnp.sum(cnt, axis=0, keepdims=True)

    zero = jnp.zeros((1, tq), jnp.int32)
    thr0 = jnp.where(count_ge(zero) >= topk, zero, jnp.full((1, tq), INT_MIN, jnp.int32))

    def bit_step(it, t):
        cand = t | jnp.left_shift(jnp.int32(1), 30 - it)
        return jnp.where(count_ge(cand) >= topk, cand, t)

    thr_row = lax.fori_loop(0, 31, bit_step, thr0)
    thr = lax.bitcast_convert_type(
        jnp.broadcast_to(lax.bitcast_convert_type(thr_row, F32), (LANES, tq)).T, jnp.int32)

    m_ref[...] = jnp.full(m_ref.shape, MASKED, F32)
    l_ref[...] = jnp.zeros_like(l_ref)
    acc_ref[...] = jnp.zeros_like(acc_ref)
    parity = qb % 2

    def head_update(hd, slot):
        r = slice(hd * tq, (hd + 1) * tq)
        lg = lg_ref[slot, r, :]
        m_prev = m_ref[r, :]
        tile_max = jnp.maximum(lg[:, :LANES], lg[:, LANES:2 * LANES])
        for c in range(2, reps):
            tile_max = jnp.maximum(tile_max, lg[:, c * LANES:(c + 1) * LANES])
        m_new = jnp.maximum(m_prev, jnp.max(tile_max.astype(F32), axis=-1, keepdims=True))
        alpha = jnp.exp2(m_prev - m_new)
        p = jnp.exp2(lg - jnp.concatenate([m_new.astype(BF16)] * reps, axis=1))
        tile_sum = p[:, :LANES] + p[:, LANES:2 * LANES]
        for c in range(2, reps):
            tile_sum = tile_sum + p[:, c * LANES:(c + 1) * LANES]
        row_sum = jnp.sum(tile_sum.astype(F32), axis=-1, keepdims=True)
        l_ref[r, :] = alpha * l_ref[r, :] + row_sum
        m_ref[r, :] = m_new
        alpha_ref[slot, r, :] = alpha
        p_ref[slot, r, :] = p

    def attend(kt, near, slot):
        start = pl.multiple_of(kt * tk, tk)
        kv = ckv_ref[pl.ds(start, tk), :]
        ks = keys_ref[kt]
        sel = jnp.concatenate(
            [ks[:, c * LANES:(c + 1) * LANES] >= thr for c in range(reps)], axis=1)
        if near:
            sel = jnp.logical_and(sel, start + k_off <= q_pos)
        shift = jnp.where(sel, 0.0, MASKED)[None]
        if near:
            kind = jnp.where(kt == n_kt - 1, 1 + 2 * parity, 2 * (1 - parity))
            shift = shift + bias_ref[kind]
        logits = _dot_nt(qlat_ref[0], kv).reshape(ATT_HEADS, tq, tk) + shift
        lg_ref[slot] = logits.reshape(rows, tk).astype(BF16)
        for hd in range(ATT_HEADS):
            head_update(hd, slot)
        alpha = alpha_ref[slot]
        acc_ref[...] = (acc_ref[...] * jnp.concatenate([alpha] * (KV_LORA // LANES), axis=1)
                        + _dot(p_ref[slot], kv))

    n_far = jnp.maximum(n_kt - 2, 0)
    n_pairs = n_far // 2

    def far_pair(pair, carry):
        attend(2 * pair, False, 0)
        attend(2 * pair + 1, False, 1)
        return carry

    def far_tile(kt, carry):
        attend(kt, False, 0)
        return carry

    def near_tile(kt, carry):
        attend(kt, True, 0)
        return carry

    lax.fori_loop(0, n_pairs, far_pair, 0)
    lax.fori_loop(2 * n_pairs, n_far, far_tile, 0)
    lax.fori_loop(n_far, n_kt, near_tile, 0)

    inv = 1.0 / l_ref[...]
    out = acc_ref[...] * jnp.concatenate([inv] * (KV_LORA // LANES), axis=1)
    for hd in range(ATT_HEADS):
        o_ref[:, hd * KV_LORA:(hd + 1) * KV_LORA] = out[hd * tq:(hd + 1) * tq, :].astype(BF16)


def _dsa_attn(qlat, qidx, tail, ckv, kidx, bias, *, batch, seq):
    tq, tk = DSA_TQ, DSA_TK
    assert seq % tk == 0 and tq == LANES and tk == 2 * tq
    nqb = seq // tq
    rows = ATT_HEADS * tq
    topk = min(TOPK_MAX, seq // 4)
    tokens = batch * seq
    qlat = qlat.reshape(batch * nqb, rows, KV_LORA)
    qidx = qidx.reshape(batch * nqb, rows, LANES)
    kern = functools.partial(_dsa_attn_kernel, topk=topk)
    return pl.pallas_call(
        kern,
        grid=(batch, nqb),
        in_specs=[
            pl.BlockSpec((1, rows, KV_LORA), lambda b, q: (b * nqb + q, 0, 0)),
            pl.BlockSpec((1, rows, LANES), lambda b, q: (b * nqb + q, 0, 0)),
            pl.BlockSpec((tq, LANES), lambda b, q: (b * nqb + q, 0)),
            pl.BlockSpec((seq, KV_LORA), lambda b, q: (b, 0)),
            pl.BlockSpec((seq, LANES), lambda b, q: (b, 0)),
            _const_spec((4, ATT_HEADS, tq, tk)),
        ],
        out_specs=pl.BlockSpec((tq, ATT_HEADS * KV_LORA), lambda b, q: (b * nqb + q, 0)),
        out_shape=jax.ShapeDtypeStruct((tokens, ATT_HEADS * KV_LORA), BF16),
        scratch_shapes=[
            pltpu.VMEM((seq // tk, tq, tk), jnp.int32),
            pltpu.VMEM((seq // tk, tk, tq), jnp.int32),
            pltpu.VMEM((rows, LANES), F32),
            pltpu.VMEM((rows, LANES), F32),
            pltpu.VMEM((rows, LANES), F32),
            pltpu.VMEM((2, rows, LANES), F32),
            pltpu.VMEM((rows, KV_LORA), F32),
            pltpu.VMEM((2, rows, tk), BF16),
            pltpu.VMEM((2, rows, tk), BF16),
        ],
        compiler_params=_params("parallel", "arbitrary"),
        name="dsa_attn",
    )(qlat, qidx, tail, ckv, kidx, bias)


def _dsa_post_kernel(olat_ref, h_ref, wuv_ref, wo_ref, o_ref, heads_ref):
    for hd in range(ATT_HEADS):
        oh = _dot(olat_ref[:, hd * KV_LORA:(hd + 1) * KV_LORA], wuv_ref[hd])
        heads_ref[:, hd * ATT_HEAD_DIM:(hd + 1) * ATT_HEAD_DIM] = oh.astype(BF16)
    o_ref[...] = h_ref[...] + _dot(heads_ref[...], wo_ref[...])


def _dsa_post(olat, h, w_uv, w_o, tm=512):
    tokens, d = h.shape
    hq = ATT_HEADS * ATT_HEAD_DIM
    return pl.pallas_call(
        _dsa_post_kernel,
        grid=(tokens // tm,),
        in_specs=[
            pl.BlockSpec((tm, ATT_HEADS * KV_LORA), lambda i: (i, 0)),
            pl.BlockSpec((tm, d), lambda i: (i, 0)),
            _const_spec((ATT_HEADS, KV_LORA, ATT_HEAD_DIM)),
            _const_spec((hq, d)),
        ],
        out_specs=pl.BlockSpec((tm, d), lambda i: (i, 0)),
        out_shape=jax.ShapeDtypeStruct((tokens, d), F32),
        scratch_shapes=[pltpu.VMEM((tm, hq), BF16)],
        compiler_params=_params("parallel"),
        name="dsa_post",
    )(olat, h, w_uv, w_o)


def _dsa_layer(h, g, rel_bias, w_in, g_q, g_kv, w_uq, w_qidx, w_uk, w_uv, w_o, *, batch, seq):
    d = h.shape[1]
    w_in_p = jnp.pad(w_in, ((0, 0), (0, A_IN_PAD - w_in.shape[1]))).astype(BF16)
    w_qidx_p = jnp.pad(w_qidx.reshape(Q_LORA, IDX_HEADS, IDX_DIM),
                       ((0, 0), (0, 0), (0, LANES - IDX_DIM))).reshape(Q_LORA, IDX_HEADS * LANES)
    qlat, qidx, ckv, kidx, tail = _dsa_pre(
        h, g, w_in_p, g_q, g_kv, w_uq.astype(BF16), w_uk.astype(BF16), w_qidx_p.astype(BF16))
    bias = _bias_tiles(rel_bias)
    olat = _dsa_attn(qlat, qidx, tail, ckv, kidx, bias, batch=batch, seq=seq)
    return _dsa_post(olat, h, w_uv.astype(BF16), w_o.astype(BF16))


def _hg_proj_kernel(h_ref, g_ref, w_ref, lb_ref, o_ref, xn_ref, *, blocks_per_part):
    j = pl.program_id(1)

    @pl.when(j == 0)
    def _():
        xn_ref[...] = _rms(h_ref[...], g_ref[...]).astype(BF16)

    part = j // blocks_per_part

    @pl.when(jnp.logical_or(part == 0, part == 3))
    def _():
        o_ref[...] = _silu(_dot(xn_ref[...], w_ref[...]))

    @pl.when(part == 1)
    def _():
        lb = lb_ref[...]
        o_ref[...] = lb + (1.0 - lb) * _sigmoid(_dot(xn_ref[...], w_ref[...]))

    @pl.when(part == 2)
    def _():
        o_ref[...] = _dot(xn_ref[...], w_ref[...])


def _hg_proj(h, g, w_in, lb, tm=512, tn=2048):
    tokens, d = h.shape
    n = w_in.shape[1]
    bpp = d // tn
    kern = functools.partial(_hg_proj_kernel, blocks_per_part=bpp)
    return pl.pallas_call(
        kern,
        grid=(tokens // tm, n // tn),
        in_specs=[
            pl.BlockSpec((tm, d), lambda i, j: (i, 0)),
            pl.BlockSpec((1, d), lambda i, j: (0, 0)),
            pl.BlockSpec((d, tn), lambda i, j: (0, j)),
            pl.BlockSpec((1, tn), lambda i, j: (0, j % bpp)),
        ],
        out_specs=pl.BlockSpec((tm, tn), lambda i, j: (i, j)),
        out_shape=jax.ShapeDtypeStruct((tokens, n), F32),
        scratch_shapes=[pltpu.VMEM((tm, d), BF16)],
        compiler_params=_params("parallel", "arbitrary"),
        name="hgrn2_proj",
    )(h, g.reshape(1, d), w_in, lb.reshape(1, d))


def _hg_rec_kernel(q_ref, f_ref, v_ref, tri_ref, o_ref, a_ref, state_ref, as_ref, ks_ref, vs_ref):
    tt = pl.program_id(1)
    rows, width = q_ref.shape
    dk = width // HG_HEADS
    sub = HG_SUB

    @pl.when(tt == 0)
    def _():
        state_ref[...] = jnp.zeros_like(state_ref)

    log_f = jnp.log(f_ref[...])
    tri = tri_ref[...]
    a_sum = None
    for _ in range(3):
        part = log_f.astype(BF16)
        log_f = log_f - part.astype(F32)
        term = _dot(tri, part)
        a_sum = term if a_sum is None else a_sum + term
    a_ref[...] = a_sum

    def step(j, carry):
        r0 = pl.multiple_of(j * sub, sub)
        groups = sub // SUBLANES
        as_ref[...] = a_ref[pl.ds(r0, sub), :]
        ks_ref[...] = 1.0 - f_ref[pl.ds(r0, sub), :]
        vs_ref[...] = v_ref[pl.ds(r0, sub), :]
        for hd in range(HG_HEADS):
            sl = slice(hd * dk, (hd + 1) * dk)
            a = as_ref[:, sl]
            q = q_ref[pl.ds(r0, sub), sl]
            k = ks_ref[:, sl]
            v = vs_ref[:, sl]
            a_last = a[sub - 1:sub, :]

            intra = [jnp.zeros((SUBLANES, dk), F32) for _ in range(groups)]
            for s in range(sub):
                g0 = s // SUBLANES
                lo = g0 * SUBLANES
                t_idx = lo + lax.broadcasted_iota(jnp.int32, (sub - lo, 1), 0)
                a_s = as_ref[s:s + 1, sl]
                k_s = ks_ref[s:s + 1, sl]
                v_s = vs_ref[s:s + 1, sl]
                e = jnp.exp(jnp.where(t_idx >= s, a[lo:, :] - a_s, -jnp.inf))
                w = q[lo:, :] * (k_s * e)
                upd = jnp.sum(w, axis=-1, keepdims=True) * v_s
                for g in range(g0, groups):
                    rows_g = slice((g - g0) * SUBLANES, (g - g0 + 1) * SUBLANES)
                    intra[g] = intra[g] + upd[rows_g, :]

            st = state_ref[hd]
            q_dec = (q * jnp.exp(a)).astype(BF16)
            k_dec = (k * jnp.exp(a_last - a)).astype(BF16)
            inter = _dot_nt(q_dec, st.astype(BF16))
            o_ref[pl.ds(r0, sub), sl] = inter + jnp.concatenate(intra, axis=0)
            state_ref[hd] = st * jnp.exp(a_last) + _dot_tn(v.astype(BF16), k_dec)
        return carry

    lax.fori_loop(0, rows // sub, step, 0)


def _hg_rec(proj, *, batch, seq, d, tt=256):
    tokens = batch * seq
    tt = min(tt, seq)
    nt = seq // tt
    dk = d // HG_HEADS
    tri = (np.arange(tt)[:, None] >= np.arange(tt)[None, :]) & (
        np.arange(tt)[:, None] // HG_SUB == np.arange(tt)[None, :] // HG_SUB)
    tri = jnp.asarray(tri.astype(np.float32)).astype(BF16)
    return pl.pallas_call(
        _hg_rec_kernel,
        grid=(batch, nt),
        in_specs=[
            pl.BlockSpec((tt, d), lambda b, t: (b * nt + t, 0)),
            pl.BlockSpec((tt, d), lambda b, t: (b * nt + t, 1)),
            pl.BlockSpec((tt, d), lambda b, t: (b * nt + t, 2)),
            _const_spec((tt, tt)),
        ],
        out_specs=pl.BlockSpec((tt, d), lambda b, t: (b * nt + t, 0)),
        out_shape=jax.ShapeDtypeStruct((tokens, d), F32),
        scratch_shapes=[pltpu.VMEM((tt, d), F32), pltpu.VMEM((HG_HEADS, dk, dk), F32)]
        + [pltpu.VMEM((HG_SUB, d), F32)] * 3,
        compiler_params=_params("parallel", "arbitrary"),
        name="hgrn2_recurrence",
    )(proj, proj, proj, tri)


def _hg_post_kernel(o_ref, gate_ref, h_ref, gn_ref, w_ref, out_ref, y_ref):
    width = o_ref.shape[1]
    dv = width // HG_HEADS
    for hd in range(HG_HEADS):
        sl = slice(hd * dv, (hd + 1) * dv)
        o = o_ref[:, sl]
        o = o * lax.rsqrt(jnp.mean(o * o, axis=-1, keepdims=True) + EPS)
        y_ref[:, sl] = (o * gn_ref[:, sl] * gate_ref[:, sl]).astype(BF16)
    out_ref[...] = h_ref[...] + _dot(y_ref[...], w_ref[...])


def _hg_post(o, proj, h, g_norm, w_o, tm=512):
    tokens, d = h.shape
    return pl.pallas_call(
        _hg_post_kernel,
        grid=(tokens // tm,),
        in_specs=[
            pl.BlockSpec((tm, d), lambda i: (i, 0)),
            pl.BlockSpec((tm, d), lambda i: (i, 3)),
            pl.BlockSpec((tm, d), lambda i: (i, 0)),
            _const_spec((1, d)),
            _const_spec((d, d)),
        ],
        out_specs=pl.BlockSpec((tm, d), lambda i: (i, 0)),
        out_shape=jax.ShapeDtypeStruct((tokens, d), F32),
        scratch_shapes=[pltpu.VMEM((tm, d), BF16)],
        compiler_params=_params("parallel"),
        name="hgrn2_post",
    )(o, proj, h, g_norm.reshape(1, d), w_o)


def _hgrn2_layer(h, g, w_in, lb, g_norm, w_o, *, batch, seq):
    d = h.shape[1]
    proj = _hg_proj(h, g, w_in.astype(BF16), lb)
    o = _hg_rec(proj, batch=batch, seq=seq, d=d)
    return _hg_post(o, proj, h, g_norm, w_o.astype(BF16))


def _lower_bounds_kernel(b_ref, o_ref):
    b = b_ref[...]
    e = jnp.exp(b - jnp.max(b, axis=0, keepdims=True))
    soft = e / jnp.sum(e, axis=0, keepdims=True)
    run = jnp.zeros_like(soft[0:1])
    for layer in range(soft.shape[0]):
        if layer > 0:
            run = run + soft[layer:layer + 1]
        o_ref[layer:layer + 1, :] = run


def _lower_bounds(b):
    return pl.pallas_call(
        _lower_bounds_kernel,
        out_shape=jax.ShapeDtypeStruct(b.shape, F32),
        name="hgrn2_lower_bounds",
    )(b)


POOL_HALO = 16


def _pool_kernel(h_ref, halo_ref, g_ref, w_ref, sc_ref, o_ref, *, tiles_per_seq):
    i = pl.program_id(0)
    tm, d = h_ref.shape
    group = d // len(POOL_WINDOWS)
    g = g_ref[...]
    u = _rms(h_ref[...], g)
    hu = _rms(halo_ref[...], g) * _not_first(i, tiles_per_seq)
    ext = jnp.concatenate([hu, u], axis=0)
    pos = (i % tiles_per_seq) * tm + lax.broadcasted_iota(jnp.int32, (tm, group), 0)
    for gi, win in enumerate(POOL_WINDOWS):
        sl = slice(gi * group, (gi + 1) * group)
        s = ext[:, sl]
        span = 1
        while span < win:
            s = s + pltpu.roll(s, span, axis=0)
            span *= 2
        count = jnp.minimum(pos + 1, win).astype(F32)
        diff = s[POOL_HALO:, :] / count - u[:, sl]
        y = _dot(diff.astype(BF16), w_ref[gi])
        o_ref[:, sl] = h_ref[:, sl] + y * sc_ref[:, sl]


def _pool_layer(h, g, w_group, scale, *, seq, tm=256):
    tokens, d = h.shape
    tm = min(tm, seq)
    assert max(POOL_WINDOWS) <= POOL_HALO
    kern = functools.partial(_pool_kernel, tiles_per_seq=seq // tm)
    return pl.pallas_call(
        kern,
        grid=(tokens // tm,),
        in_specs=[
            pl.BlockSpec((tm, d), lambda i: (i, 0)),
            pl.BlockSpec((POOL_HALO, d), lambda i: (jnp.maximum(i * (tm // POOL_HALO) - 1, 0), 0)),
            _const_spec((1, d)),
            _const_spec(w_group.shape),
            _const_spec((1, d)),
        ],
        out_specs=pl.BlockSpec((tm, d), lambda i: (i, 0)),
        out_shape=jax.ShapeDtypeStruct((tokens, d), F32),
        compiler_params=_params("parallel"),
        name="pool_mixer",
    )(h, h, g.reshape(1, d), w_group.astype(BF16), scale.reshape(1, d))


CONF_HALO = 32


def _conf_glu_kernel(h_ref, g_ref, wa_ref, wg_ref, ba_ref, bg_ref, o_ref, xn_ref):
    @pl.when(pl.program_id(1) == 0)
    def _():
        xn_ref[...] = _rms(h_ref[...], g_ref[...]).astype(BF16)

    xn = xn_ref[...]
    a = _dot(xn, wa_ref[...]) + ba_ref[...]
    gate = _dot(xn, wg_ref[...]) + bg_ref[...]
    o_ref[...] = a * _sigmoid(gate)


def _conf_glu(h, g, w_pw1, b_pw1, tm=512, tn=1024):
    tokens, d = h.shape
    nj = d // tn
    b = b_pw1.reshape(1, -1)
    return pl.pallas_call(
        _conf_glu_kernel,
        grid=(tokens // tm, nj),
        in_specs=[
            pl.BlockSpec((tm, d), lambda i, j: (i, 0)),
            pl.BlockSpec((1, d), lambda i, j: (0, 0)),
            pl.BlockSpec((d, tn), lambda i, j: (0, j)),
            pl.BlockSpec((d, tn), lambda i, j: (0, nj + j)),
            pl.BlockSpec((1, tn), lambda i, j: (0, j)),
            pl.BlockSpec((1, tn), lambda i, j: (0, nj + j)),
        ],
        out_specs=pl.BlockSpec((tm, tn), lambda i, j: (i, j)),
        out_shape=jax.ShapeDtypeStruct((tokens, d), F32),
        scratch_shapes=[pltpu.VMEM((tm, d), BF16)],
        compiler_params=_params("parallel", "arbitrary"),
        name="conformer_glu",
    )(h, g.reshape(1, d), w_pw1, w_pw1, b, b)


def _conf_conv_kernel(u_ref, halo_ref, h_ref, wdw_ref, bdw_ref, lng_ref, lnb_ref, w2_ref, b2_ref,
                      o_ref, ext_ref, c_ref, *, tiles_per_seq):
    i = pl.program_id(0)
    tm, d = u_ref.shape
    width = wdw_ref.shape[0]
    ext_ref[:CONF_HALO, :] = halo_ref[...] * _not_first(i, tiles_per_seq)
    ext_ref[CONF_HALO:, :] = u_ref[...]
    base = CONF_HALO - (width - 1)
    n_ext = tm + CONF_HALO
    for c in range(d // LANES):
        sl = slice(c * LANES, (c + 1) * LANES)
        ext = ext_ref[:, sl]
        acc = jnp.broadcast_to(bdw_ref[:, sl], (tm, LANES))
        for phase in range(SUBLANES):
            taps = [j for j in range(width) if (base + j) % SUBLANES == phase]
            if not taps:
                continue
            shifted = ext if phase == 0 else pltpu.roll(ext, n_ext - phase, axis=0)
            for j in taps:
                off = base + j - phase
                acc = acc + wdw_ref[j:j + 1, sl] * shifted[off:off + tm, :]
        c_ref[:, sl] = acc
    x = c_ref[...]
    mu = jnp.mean(x, axis=-1, keepdims=True)
    xc = x - mu
    y = xc * lax.rsqrt(jnp.mean(xc * xc, axis=-1, keepdims=True) + EPS)
    y = _silu(y * lng_ref[...] + lnb_ref[...]).astype(BF16)
    o_ref[...] = h_ref[...] + _dot(y, w2_ref[...]) + b2_ref[...]


def _conf_conv(u, h, w_dw, b_dw, ln_g, ln_b, w_pw2, b_pw2, *, seq, tm=256):
    tokens, d = h.shape
    tm = min(tm, seq)
    width = w_dw.shape[0]
    assert width - 1 <= CONF_HALO
    kern = functools.partial(_conf_conv_kernel, tiles_per_seq=seq // tm)
    return pl.pallas_call(
        kern,
        grid=(tokens // tm,),
        in_specs=[
            pl.BlockSpec((tm, d), lambda i: (i, 0)),
            pl.BlockSpec((CONF_HALO, d), lambda i: (jnp.maximum(i * (tm // CONF_HALO) - 1, 0), 0)),
            pl.BlockSpec((tm, d), lambda i: (i, 0)),
            _const_spec((width, d)),
            _const_spec((1, d)),
            _const_spec((1, d)),
            _const_spec((1, d)),
            _const_spec((d, d)),
            _const_spec((1, d)),
        ],
        out_specs=pl.BlockSpec((tm, d), lambda i: (i, 0)),
        out_shape=jax.ShapeDtypeStruct((tokens, d), F32),
        scratch_shapes=[pltpu.VMEM((tm + CONF_HALO, d), F32), pltpu.VMEM((tm, d), F32)],
        compiler_params=_params("parallel"),
        name="conformer_conv",
    )(u, u, h, w_dw, b_dw.reshape(1, d), ln_g.reshape(1, d), ln_b.reshape(1, d), w_pw2,
      b_pw2.reshape(1, d))


def _conformer_layer(h, g, w_pw1, b_pw1, w_dw, b_dw, ln_g, ln_b, w_pw2, b_pw2, *, seq):
    u = _conf_glu(h, g, w_pw1.astype(BF16), b_pw1)
    return _conf_conv(u, h, w_dw, b_dw, ln_g, ln_b, w_pw2.astype(BF16), b_pw2, seq=seq)


def kernel(x, rel_bias, a_w_in, a_g_q, a_g_kv, a_w_uq, a_w_qidx, a_w_uk, a_w_uv, a_w_o, b_w_in, b_lower_bounds, b_g_norm, b_w_o, c_w_group, c_scale, d_w_pw1, d_b_pw1, d_w_dw, d_b_dw, d_ln_g, d_ln_b, d_w_pw2, d_b_pw2, norm_mix, norm_ffn, ffn_w_up, ffn_w_conv, ffn_b_conv, ffn_w_down, final_norm):
    batch, seq, d = x.shape
    depth = norm_mix.shape[0]
    lb_all = _lower_bounds(b_lower_bounds)
    h = x.reshape(batch * seq, d)
    for i in range(depth):
        j, kind = divmod(i, 4)
        g = norm_mix[i]
        if kind == 0:
            h = _dsa_layer(h, g, rel_bias, a_w_in[j], a_g_q[j], a_g_kv[j], a_w_uq[j], a_w_qidx[j],
                           a_w_uk[j], a_w_uv[j], a_w_o[j], batch=batch, seq=seq)
        elif kind == 1:
            h = _hgrn2_layer(h, g, b_w_in[j], lb_all[i], b_g_norm[j], b_w_o[j], batch=batch, seq=seq)
        elif kind == 2:
            h = _pool_layer(h, g, c_w_group[j], c_scale[j], seq=seq)
        else:
            h = _conformer_layer(h, g, d_w_pw1[j], d_b_pw1[j], d_w_dw[j], d_b_dw[j], d_ln_g[j],
                                 d_ln_b[j], d_w_pw2[j], d_b_pw2[j], seq=seq)
        h = _ffn_layer(h, norm_ffn[i], ffn_w_up[i].astype(BF16), ffn_w_conv[i], ffn_b_conv[i],
                       ffn_w_down[i].astype(BF16), final_norm, seq=seq,
                       final_norm=(i == depth - 1))
    return h.reshape(batch, seq, d)
```
